```python
import jax
import jax.numpy as jnp
from jax import lax
import numpy as np

D_MODEL = 1024
BATCH = 16
SEQ = 2048
DEPTH = 2

CTX_LEN = 256
GRID_W = 64
D_MIX = D_MODEL
HEAD_DIM = 64
D_POOL = D_MIX // 4
D_RWKV = (D_MIX - D_POOL) // 2
D_HGRN = D_MIX - D_POOL - D_RWKV
POOL_WINDOWS = (2, 4, 8, 16)
N_POOL_GROUPS = len(POOL_WINDOWS)
POOL_GROUP = D_POOL // N_POOL_GROUPS
H_RWKV = D_RWKV // HEAD_DIM
H_HGRN = D_HGRN // HEAD_DIM
LORA_W = 64
LORA_A = 64
N_DIR = 2
HGRN_CHUNK = 32
D_RWKV_IN = 3 * D_RWKV + N_DIR * (LORA_W + LORA_A)
D_IN = D_POOL + D_RWKV_IN + 2 * D_HGRN + N_DIR * D_HGRN + D_MIX
ALPHA = float((2 * DEPTH) ** 0.25)
BETA = float((8 * DEPTH) ** -0.25)
LN_EPS = 1e-5
GN_EPS = 64e-5
RMS_EPS = 1e-6

kernel_name = "hybrid_pool_rwkv7_hgrn2_flow_block"


def _layer_norm(x):
    xf = x.astype(jnp.float32)
    mu = jnp.mean(xf, axis=-1, keepdims=True)
    var = jnp.mean(jnp.square(xf - mu), axis=-1, keepdims=True)
    return (xf - mu) * lax.rsqrt(var + LN_EPS)


def _heads(z):
    return z.reshape(z.shape[:-1] + (z.shape[-1] // HEAD_DIM, HEAD_DIM))


def _split_proj(p):
    o0 = D_POOL
    o1 = o0 + D_RWKV_IN
    o2 = o1 + 2 * D_HGRN + N_DIR * D_HGRN
    return p[..., :o0], p[..., o0:o1], p[..., o1:o2], p[..., o2:]


def _centred_mean(u, axis, window):
    n = u.shape[axis]
    left = window // 2
    right = window - 1 - left
    t = jnp.arange(n)
    lo = jnp.maximum(t - left, 0)
    hi = jnp.minimum(t + right, n - 1) + 1
    pad = [(0, 0)] * u.ndim
    pad[axis] = (1, 0)
    cs = jnp.pad(jnp.cumsum(u, axis=axis), pad)
    total = jnp.take(cs, hi, axis=axis) - jnp.take(cs, lo, axis=axis)
    shape = [1] * u.ndim
    shape[axis] = n
    count = (hi - lo).astype(jnp.float32).reshape(shape)
    return total / count


def _pool_branch(u, grid, pool_w, pool_scale):
    b, n, _ = u.shape
    uf = u.astype(jnp.float32)
    if grid:
        rows = n // GRID_W
        uf = uf.reshape(b, rows, GRID_W, D_POOL)
        axis = 2
    else:
        axis = 1
    groups = []
    for g, win in enumerate(POOL_WINDOWS):
        ch = uf[..., g * POOL_GROUP:(g + 1) * POOL_GROUP]
        groups.append(_centred_mean(ch, axis, win) - ch)
    pooled = jnp.stack(groups, axis=-2)
    mixed = jnp.einsum('...gi,gio->...go', pooled, pool_w)
    return mixed.reshape(b, n, D_POOL) * pool_scale


def _shift3(u, w):
    up = jnp.pad(u, ((0, 0), (1, 1), (0, 0)))
    return w[0] * up[:, :-2] + w[1] * up[:, 1:-1] + w[2] * up[:, 2:]


def _rwkv_dir_inputs(u, d, w0, w_up, a0, a_up, k_k, k_a):
    r = u[..., :D_RWKV]
    k = u[..., D_RWKV:2 * D_RWKV]
    v = u[..., 2 * D_RWKV:3 * D_RWKV]
    o = 3 * D_RWKV
    xw = u[..., o + d * LORA_W:o + (d + 1) * LORA_W]
    o2 = o + N_DIR * LORA_W
    xa = u[..., o2 + d * LORA_A:o2 + (d + 1) * LORA_A]
    w_log = -jax.nn.softplus(-(w0[d] + jnp.tanh(xw) @ w_up[d])) - 0.5
    decay = jnp.exp(-jnp.exp(w_log))
    a = jax.nn.sigmoid(a0[d] + xa @ a_up[d])
    kk = _heads(k * k_k)
    kk = kk * lax.rsqrt(jnp.sum(kk * kk, axis=-1, keepdims=True) + 1e-12)
    k_mod = k * (1.0 + (a - 1.0) * k_a)
    return (_heads(r), _heads(decay), _heads(k_mod), _heads(v), kk, _heads(a))


def _rwkv_bonus(ins, r_k_d):
    r, k_mod, v = ins[0], ins[2], ins[3]
    return jnp.sum(r * k_mod * r_k_d, axis=-1, keepdims=True) * v


def _rwkv_scan(state, ins, emit):
    r, w, k, v, kk, a = (jnp.moveaxis(z, 1, 0) for z in ins)
    xs = (w, k, v, kk, a, r) if emit else (w, k, v, kk, a)

    def step(S, inp):
        w_t, k_t, v_t, kk_t, a_t = inp[:5]
        sa = jnp.einsum('bhvk,bhk->bhv', S, kk_t)
        S = (S * w_t[:, :, None, :] - sa[..., None] * (kk_t * a_t)[:, :, None, :]
             + v_t[..., None] * k_t[:, :, None, :])
        y = jnp.einsum('bhvk,bhk->bhv', S, inp[5]) if emit else None
        return S, y

    state, ys = lax.scan(step, state, xs)
    return state, (jnp.moveaxis(ys, 0, 1) if emit else None)


def _rwkv_readout(y, bonus, gn_g, gn_b):
    mu = jnp.mean(y, axis=-1, keepdims=True)
    var = jnp.mean(jnp.square(y - mu), axis=-1, keepdims=True)
    yn = (y - mu) * lax.rsqrt(var + GN_EPS) * gn_g.reshape(H_RWKV, HEAD_DIM) + gn_b.reshape(H_RWKV, HEAD_DIM)
    out = yn + bonus
    return out.reshape(out.shape[:-2] + (D_RWKV,))


def _rwkv_branch(u_ctx, u_lat, w0, w_up, a0, a_up, k_k, k_a, r_k, gn_g, gn_b, emit_ctx):
    b = u_lat.shape[0]
    y_ctx, y_lat, bonus_ctx, bonus_lat = 0.0, 0.0, 0.0, 0.0
    for d in range(N_DIR):
        ins_c = _rwkv_dir_inputs(u_ctx, d, w0, w_up, a0, a_up, k_k, k_a)
        ins_l = _rwkv_dir_inputs(u_lat, d, w0, w_up, a0, a_up, k_k, k_a)
        bonus_lat = bonus_lat + _rwkv_bonus(ins_l, r_k[d])
        if emit_ctx:
            bonus_ctx = bonus_ctx + _rwkv_bonus(ins_c, r_k[d])
        if d == 1:
            ins_c = tuple(jnp.flip(z, axis=1) for z in ins_c)
            ins_l = tuple(jnp.flip(z, axis=1) for z in ins_l)
        s0 = jnp.zeros((b, H_RWKV, HEAD_DIM, HEAD_DIM), jnp.float32)
        s_c, yc = _rwkv_scan(s0, ins_c, emit_ctx)
        _, yl = _rwkv_scan(s_c, ins_l, True)
        if d == 1:
            yl = jnp.flip(yl, axis=1)
            if emit_ctx:
                yc = jnp.flip(yc, axis=1)
        y_lat = y_lat + yl
        if emit_ctx:
            y_ctx = y_ctx + yc
    out_lat = _rwkv_readout(y_lat, bonus_lat, gn_g, gn_b)
    out_ctx = _rwkv_readout(y_ctx, bonus_ctx, gn_g, gn_b) if emit_ctx else None
    return out_ctx, out_lat


def _hgrn_dir_inputs(fh, d, lb_d):
    q = fh[..., :D_HGRN]
    i = fh[..., D_HGRN:2 * D_HGRN]
    z = fh[..., (2 + d) * D_HGRN:(3 + d) * D_HGRN]
    logf = jnp.logaddexp(jnp.log(lb_d), jnp.log1p(-lb_d) + jax.nn.log_sigmoid(z))
    k = (1.0 - lb_d) * jax.nn.sigmoid(-z)
    return (_heads(k), _heads(i), _heads(logf), _heads(q))


def _hgrn_scan(state, ins, emit):
    k, i, logf, q = ins
    b, n, h, dk = k.shape
    nc = n // HGRN_CHUNK

    def chunks(z):
        return z.reshape(b, nc, HGRN_CHUNK, h, dk).transpose(1, 0, 3, 2, 4)

    causal = jnp.tril(jnp.ones((HGRN_CHUNK, HGRN_CHUNK), dtype=bool))[:, :, None]

    def step(S, inp):
        k_c, i_c, lf_c = inp[:3]
        g = jnp.cumsum(lf_c, axis=2)
        g_last = g[:, :, -1:, :]
        if emit:
            q_c = inp[3]
            inter = jnp.einsum('bhtk,bhkv->bhtv', q_c * jnp.exp(g), S)
            rel = jnp.exp(jnp.where(causal, g[:, :, :, None, :] - g[:, :, None, :, :], -jnp.inf))
            scores = jnp.einsum('bhtk,bhsk,bhtsk->bhts', q_c, k_c, rel)
            o = inter + jnp.einsum('bhts,bhsv->bhtv', scores, i_c)
        else:
            o = None
        S = (jnp.exp(g_last[:, :, 0, :])[..., None] * S
             + jnp.einsum('bhsk,bhsv->bhkv', k_c * jnp.exp(g_last - g), i_c))
        return S, o

    xs = (chunks(k), chunks(i), chunks(logf), chunks(q)) if emit else (chunks(k), chunks(i), chunks(logf))
    state, o = lax.scan(step, state, xs)
    if emit:
        o = o.transpose(1, 0, 3, 2, 4).reshape(b, n, h, dk)
    return state, o


def _rms_heads(o, norm_g):
    on = o * lax.rsqrt(jnp.mean(o * o, axis=-1, keepdims=True) + RMS_EPS) * norm_g.reshape(H_HGRN, HEAD_DIM)
    return on.reshape(on.shape[:-2] + (D_HGRN,))


def _hgrn_branch(fh_ctx, fh_lat, lb, norm_g, emit_ctx):
    b = fh_lat.shape[0]
    o_ctx, o_lat = 0.0, 0.0
    for d in range(N_DIR):
        ins_c = _hgrn_dir_inputs(fh_ctx, d, lb[d])
        ins_l = _hgrn_dir_inputs(fh_lat, d, lb[d])
        if d == 1:
            ins_c = tuple(jnp.flip(z, axis=1) for z in ins_c)
            ins_l = tuple(jnp.flip(z, axis=1) for z in ins_l)
        s0 = jnp.zeros((b, H_HGRN, HEAD_DIM, HEAD_DIM), jnp.float32)
        s_c, oc = _hgrn_scan(s0, ins_c, emit_ctx)
        _, ol = _hgrn_scan(s_c, ins_l, True)
        if d == 1:
            ol = jnp.flip(ol, axis=1)
            if emit_ctx:
                oc = jnp.flip(oc, axis=1)
        o_lat = o_lat + ol
        if emit_ctx:
            o_ctx = o_ctx + oc
    out_ctx = _rms_heads(o_ctx, norm_g) if emit_ctx else None
    return out_ctx, _rms_heads(o_lat, norm_g)


def setup_inputs(seed: int = 0) -> dict:
    key = jax.random.key(seed)
    ks = jax.random.split(key, 24)
    f32 = jnp.float32

    def nrm(k, shape, scale):
        return scale * jax.random.normal(k, shape, f32)

    return {
        "x": nrm(ks[0], (BATCH, SEQ, D_MODEL), 1.0),
        "c": nrm(ks[1], (BATCH, D_MODEL), 1.0),
        "ctx": nrm(ks[2], (BATCH, CTX_LEN, D_MODEL), 1.0),
        "c_ctx": nrm(ks[3], (D_MODEL,), 1.0),
        "mod_w": nrm(ks[4], (DEPTH, D_MODEL, 3 * D_MODEL), D_MODEL ** -0.5),
        "mod_b": nrm(ks[5], (DEPTH, 3 * D_MODEL), 0.02),
        "w_in": nrm(ks[6], (DEPTH, D_MODEL, D_IN), D_MODEL ** -0.5),
        "rwkv_shift": jnp.array([0.25, 0.5, 0.25], f32)[None, :, None] + nrm(ks[7], (DEPTH, 3, D_RWKV_IN), 0.05),
        "pool_w": nrm(ks[8], (DEPTH, N_POOL_GROUPS, POOL_GROUP, POOL_GROUP), POOL_GROUP ** -0.5),
        "pool_scale": 1.0 + nrm(ks[9], (DEPTH, D_POOL), 0.1),
        "rwkv_w0": jax.random.uniform(ks[10], (DEPTH, N_DIR, D_RWKV), f32, minval=-5.0, maxval=-0.5),
        "rwkv_w_up": nrm(ks[11], (DEPTH, N_DIR, LORA_W, D_RWKV), 0.5 * LORA_W ** -0.5),
        "rwkv_a0": nrm(ks[12], (DEPTH, N_DIR, D_RWKV), 0.1),
        "rwkv_a_up": nrm(ks[13], (DEPTH, N_DIR, LORA_A, D_RWKV), LORA_A ** -0.5),
        "rwkv_k_k": 0.85 + nrm(ks[14], (DEPTH, D_RWKV), 0.05),
        "rwkv_k_a": 1.0 + nrm(ks[15], (DEPTH, D_RWKV), 0.05),
        "rwkv_r_k": nrm(ks[16], (DEPTH, N_DIR, H_RWKV, HEAD_DIM), 0.1),
        "rwkv_gn_g": 1.0 + nrm(ks[17], (DEPTH, D_RWKV), 0.05),
        "rwkv_gn_b": nrm(ks[18], (DEPTH, D_RWKV), 0.02),
        "hgrn_lb_logits": nrm(ks[19], (N_DIR, DEPTH, D_HGRN), 0.5),
        "hgrn_norm_g": 1.0 + nrm(ks[20], (DEPTH, D_HGRN), 0.05),
        "w_out": nrm(ks[21], (DEPTH, D_MIX, D_MODEL), BETA * D_MIX ** -0.5),
        "ln_g": 1.0 + nrm(ks[22], (DEPTH, D_MODEL), 0.05),
        "ln_b": nrm(ks[23], (DEPTH, D_MODEL), 0.02),
    }


def reference(x, c, ctx, c_ctx, mod_w, mod_b, w_in, rwkv_shift, pool_w, pool_scale,
              rwkv_w0, rwkv_w_up, rwkv_a0, rwkv_a_up, rwkv_k_k, rwkv_k_a, rwkv_r_k,
              rwkv_gn_g, rwkv_gn_b, hgrn_lb_logits, hgrn_norm_g, w_out, ln_g, ln_b):
    out_dtype = x.dtype
    lb_w = jax.nn.softmax(hgrn_lb_logits.astype(jnp.float32), axis=1)
    lb_all = jnp.maximum(jnp.cumsum(lb_w, axis=1) - lb_w[:, :1], 0.0)
    h_x = x.astype(jnp.float32)
    h_ctx = ctx.astype(jnp.float32)
    for l in range(DEPTH):
        last = l == DEPTH - 1
        mod_lat = jax.nn.silu(c.astype(jnp.float32)) @ mod_w[l] + mod_b[l]
        mod_c = jax.nn.silu(c_ctx.astype(jnp.float32)) @ mod_w[l] + mod_b[l]
        sh_l, sc_l, gt_l = jnp.split(mod_lat, 3, axis=-1)
        sh_c, sc_c, gt_c = jnp.split(mod_c, 3, axis=-1)
        p_lat = (_layer_norm(h_x) * (1.0 + sc_l[:, None]) + sh_l[:, None]) @ w_in[l]
        p_ctx = (_layer_norm(h_ctx) * (1.0 + sc_c) + sh_c) @ w_in[l]
        pv_l, u_l, fh_l, g_l = _split_proj(p_lat)
        pv_c, u_c, fh_c, g_c = _split_proj(p_ctx)
        u_l = _shift3(u_l, rwkv_shift[l])
        u_c = _shift3(u_c, rwkv_shift[l])
        y_rc, y_rl = _rwkv_branch(u_c, u_l, rwkv_w0[l], rwkv_w_up[l], rwkv_a0[l], rwkv_a_up[l],
                                  rwkv_k_k[l], rwkv_k_a[l], rwkv_r_k[l], rwkv_gn_g[l], rwkv_gn_b[l],
                                  not last)
        o_hc, o_hl = _hgrn_branch(fh_c, fh_l, lb_all[:, l], hgrn_norm_g[l], not last)
        pool_l = _pool_branch(pv_l, True, pool_w[l], pool_scale[l])
        mix_l = jnp.concatenate([pool_l, y_rl, o_hl], axis=-1) * jax.nn.silu(g_l)
        new_x = _layer_norm(ALPHA * h_x + gt_l[:, None] * (mix_l @ w_out[l])) * ln_g[l] + ln_b[l]
        if not last:
            pool_c = _pool_branch(pv_c, False, pool_w[l], pool_scale[l])
            mix_c = jnp.concatenate([pool_c, y_rc, o_hc], axis=-1) * jax.nn.silu(g_c)
            h_ctx = _layer_norm(ALPHA * h_ctx + gt_c * (mix_c @ w_out[l])) * ln_g[l] + ln_b[l]
        h_x = new_x
    return h_x.astype(out_dtype)
```

```python
import functools

import numpy as np
import jax
import jax.numpy as jnp
from jax import lax
from jax.experimental import pallas as pl
from jax.experimental.pallas import tpu as pltpu

D_MODEL = 1024
CTX_LEN = 256
GRID_W = 64
HEAD_DIM = 64
D_POOL = 256
D_RWKV = 384
D_HGRN = 384
POOL_WINDOWS = (2, 4, 8, 16)
POOL_GROUP = 64
LORA = 64
D_IN = 4224
LN_EPS = 1e-5
GN_EPS = 64e-5
RMS_EPS = 1e-6

TOK_BLOCK = 256
CHUNK = 64
PAIR = 2 * HEAD_DIM
N_PAIR = D_RWKV // PAIR
N_CHUNK = TOK_BLOCK // CHUNK
CHUNK_LEVELS = (1, 2, 4, 8, 16, 32)
HALO = 8
VMEM_LIMIT_BYTES = 56 * 1024 * 1024

COL_GATE = 0
COL_POOL = 1024
COL_LORA = 1280
COL_HEADS = 1536

_F32 = jnp.float32
_HI = lax.Precision.HIGHEST


def _softplus(x):
    return jnp.maximum(x, 0.0) + jnp.log1p(jnp.exp(-jnp.abs(x)))


def _dot(a, b, precision=_HI):
    return jnp.dot(a, b, precision=precision, preferred_element_type=_F32)


def _dot_nt(a, b, precision=_HI):
    return lax.dot_general(a, b, (((1,), (1,)), ((), ())), precision=precision,
                           preferred_element_type=_F32)


def _dot_tn(a, b, precision=_HI):
    return lax.dot_general(a, b, (((0,), (0,)), ((), ())), precision=precision,
                           preferred_element_type=_F32)


def _chunk_pos(direction):
    t = np.arange(CHUNK)
    return t if direction == 0 else CHUNK - 1 - t


def _level_tables(direction):
    pos = _chunk_pos(direction)
    masks, sums = [], []
    for m in CHUNK_LEVELS:
        blk = pos // m
        pair = pos // (2 * m)
        late = (blk % 2) == 1
        same = pair[:, None] == pair[None, :]
        masks.append(same & late[:, None] & (~late)[None, :])
        late_rows = same & late[:, None] & late[None, :] & (pos[None, :] <= pos[:, None])
        early_rows = same & (~late)[:, None] & (~late)[None, :] & (pos[None, :] > pos[:, None])
        sums.append(late_rows | early_rows)
    return masks, sums


def _tile2(m):
    return np.tile(m.astype(np.float32), (2, 2))


@functools.lru_cache(maxsize=None)
def _scan_tables():
    rw_masks, hg_masks, cums, hg_sums = [], [], [], []
    for d in range(2):
        pos = _chunk_pos(d)
        strict = pos[None, :] < pos[:, None]
        incl = pos[None, :] <= pos[:, None]
        after = pos[None, :] > pos[:, None]
        lv_masks, lv_sums = _level_tables(d)
        eye = np.eye(CHUNK, dtype=bool)
        rw_masks.append(np.stack([_tile2(strict), _tile2(incl)] + [_tile2(m) for m in lv_masks]
                                 + [np.eye(PAIR, dtype=np.float32)]))
        hg_masks.append(np.stack([_tile2(m) for m in lv_masks] + [_tile2(eye)]))
        cums.append(incl.astype(np.float32))
        hg_sums.append(np.concatenate([incl, after] + lv_sums, axis=0).astype(np.float32))
    return (np.stack(rw_masks), np.stack(hg_masks), np.stack(cums), np.stack(hg_sums))


@functools.lru_cache(maxsize=None)
def _pool_tables():
    out = np.zeros((2, len(POOL_WINDOWS), TOK_BLOCK, TOK_BLOCK), np.float32)
    for kind, seg in enumerate((TOK_BLOCK, GRID_W)):
        for g, win in enumerate(POOL_WINDOWS):
            left = win // 2
            right = win - 1 - left
            for t in range(TOK_BLOCK):
                base = (t // seg) * seg
                lo = max(t - left, base)
                hi = min(t + right, base + seg - 1) + 1
                out[kind, g, t, lo:hi] = 1.0 / (hi - lo)
                out[kind, g, t, t] -= 1.0
    return out


@functools.lru_cache(maxsize=None)
def _head_mean_table():
    h = np.arange(D_RWKV) // HEAD_DIM
    return (h[:, None] == h[None, :]).astype(np.float32) / HEAD_DIM


@functools.lru_cache(maxsize=None)
def _column_perm():
    pool = np.arange(0, 256)
    u0 = 256
    r = u0 + np.arange(0, 384)
    k = u0 + np.arange(384, 768)
    v = u0 + np.arange(768, 1152)
    lora = []
    for d in range(2):
        lora.append(u0 + 1152 + d * LORA + np.arange(LORA))
        lora.append(u0 + 1280 + d * LORA + np.arange(LORA))
    f0 = 256 + 1408
    q = f0 + np.arange(0, 384)
    i = f0 + np.arange(384, 768)
    z0 = f0 + np.arange(768, 1152)
    z1 = f0 + np.arange(1152, 1536)
    gate = np.arange(3200, 4224)
    perm = np.concatenate([gate, pool] + lora + [r, k, v, q, i, z0, z1])
    assert perm.shape[0] == D_IN and np.unique(perm).shape[0] == D_IN
    return perm


def _mod_kernel(c_ref, w_ref, b_ref, o_ref):
    c = c_ref[...]
    o_ref[0] = _dot(c * jax.nn.sigmoid(c), w_ref[0]) + b_ref[0]


def _modulation(c_all, mod_w, mod_b):
    depth = mod_w.shape[0]
    rows = c_all.shape[0]
    n_tile = 1024
    return pl.pallas_call(
        _mod_kernel,
        grid=(depth, 3 * D_MODEL // n_tile),
        in_specs=[
            pl.BlockSpec((rows, D_MODEL), lambda l, n: (0, 0)),
            pl.BlockSpec((1, D_MODEL, n_tile), lambda l, n: (l, 0, n)),
            pl.BlockSpec((1, 1, n_tile), lambda l, n: (l, 0, n)),
        ],
        out_specs=pl.BlockSpec((1, rows, n_tile), lambda l, n: (l, 0, n)),
        out_shape=jax.ShapeDtypeStruct((depth, rows, 3 * D_MODEL), _F32),
        compiler_params=pltpu.CompilerParams(
            dimension_semantics=("arbitrary", "arbitrary"), vmem_limit_bytes=VMEM_LIMIT_BYTES),
        name="modulation",
    )(c_all, mod_w, mod_b.reshape(depth, 1, 3 * D_MODEL))


def _layer_norm(x):
    mu = jnp.mean(x, axis=-1, keepdims=True)
    xc = x - mu
    var = jnp.mean(xc * xc, axis=-1, keepdims=True)
    return xc * lax.rsqrt(var + LN_EPS)


def _proj_kernel(h_ref, sh_ref, sc_ref, w_ref, p_ref):
    xn = _layer_norm(h_ref[0]) * (1.0 + sc_ref[0]) + sh_ref[0]
    p_ref[0] = jnp.dot(xn.astype(jnp.bfloat16), w_ref[...], preferred_element_type=_F32)


def _mod_row(batch):
    return lambda b, j: (jnp.where(j == 0, batch, b), 0, 0)


def _input_projection(h, shift, scale, w_in_bf16):
    batch, n_tok, _ = h.shape
    return pl.pallas_call(
        _proj_kernel,
        grid=(batch, n_tok // TOK_BLOCK),
        in_specs=[
            pl.BlockSpec((1, TOK_BLOCK, D_MODEL), lambda b, j: (b, j, 0)),
            pl.BlockSpec((1, 1, D_MODEL), _mod_row(batch)),
            pl.BlockSpec((1, 1, D_MODEL), _mod_row(batch)),
            pl.BlockSpec((D_MODEL, D_IN), lambda b, j: (0, 0)),
        ],
        out_specs=pl.BlockSpec((1, TOK_BLOCK, D_IN), lambda b, j: (b, j, 0)),
        out_shape=jax.ShapeDtypeStruct((batch, n_tok, D_IN), _F32),
        compiler_params=pltpu.CompilerParams(
            dimension_semantics=("parallel", "parallel"), vmem_limit_bytes=VMEM_LIMIT_BYTES),
        name="input_projection",
    )(h, shift, scale, w_in_bf16)


def _scan_block(n_blk):
    return lambda d, j: jnp.where((d == 0) | (j == 0), j, n_blk - j)


def _stack_heads(x, lane_lo, lane_hi):
    return jnp.concatenate([x * lane_lo, x * lane_hi], axis=0)


def _unstack_heads(x2):
    return x2[:CHUNK] + x2[CHUNK:]


def _lane_masks():
    lane = lax.broadcasted_iota(jnp.int32, (CHUNK, PAIR), 1)
    lo = (lane < HEAD_DIM).astype(_F32)
    return lo, 1.0 - lo


def _rwkv_kernel(r_ref, k_ref, v_ref, lo_ref,
                 rp_ref, kp_ref, vp_ref, lp_ref, rn_ref, kn_ref, vn_ref, ln_ref,
                 shr_ref, shk_ref, shv_ref, shl_ref,
                 w0_ref, wup_ref, a0_ref, aup_ref, kk_ref, ka_ref, rk_ref,
                 hm_ref, mask_ref, cum_ref,
                 y_ref, bonus_ref,
                 tok_ref, state_ref, *, n_blk):
    d = pl.program_id(1)
    j = pl.program_id(2)
    blk = _scan_block(n_blk)(d, j)
    prev_ok = jnp.where((blk == 0) | (blk == 1), 0.0, 1.0).astype(_F32)
    next_ok = jnp.where((blk == 0) | (blk == n_blk - 1), 0.0, 1.0).astype(_F32)

    @pl.when(j == 0)
    def _():
        state_ref[...] = jnp.zeros_like(state_ref)

    def shifted(main_ref, prev_ref, next_ref, w_ref):
        x = main_ref[0]
        w = w_ref[0]
        row = lax.broadcasted_iota(jnp.int32, x.shape, 0)
        prev_row = prev_ref[0, HALO - 1:HALO, :] * prev_ok
        next_row = next_ref[0, 0:1, :] * next_ok
        x_prev = jnp.where(row == 0, prev_row, pltpu.roll(x, 1, axis=0))
        x_next = jnp.where(row == TOK_BLOCK - 1, next_row, pltpu.roll(x, TOK_BLOCK - 1, axis=0))
        return w[0:1] * x_prev + w[1:2] * x + w[2:3] * x_next

    r = shifted(r_ref, rp_ref, rn_ref, shr_ref)
    k = shifted(k_ref, kp_ref, kn_ref, shk_ref)
    v = shifted(v_ref, vp_ref, vn_ref, shv_ref)
    lora = shifted(lo_ref, lp_ref, ln_ref, shl_ref)

    w_pre = w0_ref[0] + _dot(jnp.tanh(lora), wup_ref[0])
    w_log = -_softplus(-w_pre) - 0.5
    lw = -jnp.exp(w_log)
    a = jax.nn.sigmoid(a0_ref[0] + _dot(lora, aup_ref[0]))
    kk = k * kk_ref[...]
    kk = kk * lax.rsqrt(_dot(kk * kk, hm_ref[...]) * HEAD_DIM + 1e-12)
    k_mod = k * (1.0 + (a - 1.0) * ka_ref[...])
    bonus_ref[0, 0] = _dot(r * k_mod * rk_ref[0], hm_ref[...]) * HEAD_DIM * v

    tok_ref[0] = lw
    tok_ref[1] = kk
    tok_ref[2] = a * kk
    tok_ref[3] = k_mod
    tok_ref[4] = v
    tok_ref[5] = r

    lane_lo, lane_hi = _lane_masks()
    m_strict = mask_ref[0, 0]
    m_incl = mask_ref[0, 1]
    eye = mask_ref[0, 2 + len(CHUNK_LEVELS)]
    zeros = jnp.zeros((PAIR, PAIR), _F32)

    for c in range(N_CHUNK):
        start = pl.multiple_of(jnp.where(d == 0, c, N_CHUNK - 1 - c) * CHUNK, CHUNK)
        rows = pl.ds(start, CHUNK)
        lw_c = tok_ref[0, rows, :]
        g_incl = _dot(cum_ref[0], lw_c)
        g_excl = g_incl - lw_c
        g_mid = g_incl[CHUNK // 2:CHUNK // 2 + 1, :]
        g_end = jnp.sum(lw_c, axis=0, keepdims=True)
        e_in = jnp.exp(g_incl - g_mid)
        e_out = jnp.exp(g_mid - g_incl)
        e_end = jnp.exp(g_end - g_incl)
        p_end = jnp.exp(g_end)
        p_mid = jnp.exp(g_mid)
        kk_c = tok_ref[1, rows, :]
        b_c = tok_ref[2, rows, :]
        k_c = tok_ref[3, rows, :]
        kkd = kk_c * jnp.exp(g_excl - g_mid)
        rd = tok_ref[5, rows, :] * e_in
        ki = k_c * e_out
        bi = b_c * e_out
        kd = k_c * e_end
        bd = b_c * e_end
        v_c = tok_ref[4, rows, :]

        for p in range(N_PAIR):
            sl = slice(p * PAIR, (p + 1) * PAIR)
            stack = lambda x: _stack_heads(x[:, sl], lane_lo, lane_hi)
            kkd2, rd2, ki2, bi2, kd2, bd2, v2 = (stack(x) for x in (kkd, rd, ki, bi, kd, bd, v_c))
            gm = _dot_nt(jnp.concatenate([kkd2, rd2], axis=0), jnp.concatenate([ki2, bi2], axis=0))
            a_k = gm[:PAIR, :PAIR] * m_strict
            a_b = gm[:PAIR, PAIR:] * m_strict
            r_k = gm[PAIR:, :PAIR] * m_incl
            r_b = gm[PAIR:, PAIR:] * m_incl
            t_inv = eye - a_b * mask_ref[0, 2]
            for lv in range(1, len(CHUNK_LEVELS)):
                t_inv = t_inv - _dot(_dot(t_inv, a_b * mask_ref[0, 2 + lv]), t_inv)
            w0 = _dot(a_k, v2)
            p_mid_p = p_mid[:, sl]
            twk = _dot(t_inv, jnp.concatenate([w0, kkd2 * p_mid_p], axis=1))
            z = jnp.concatenate([jnp.concatenate([v2, zeros], axis=1), twk], axis=0)
            yr = _dot(jnp.concatenate([r_k, -r_b], axis=1), z)
            mn = _dot_tn(jnp.concatenate([kd2, -bd2], axis=0), z)
            st = state_ref[p]
            y2 = yr[:, :PAIR] + _dot(rd2 * p_mid_p + yr[:, PAIR:], st)
            y_ref[0, 0, rows, sl] = _unstack_heads(y2)
            m_c = mn[:, PAIR:] + eye * p_end[:, sl]
            state_ref[p] = _dot(m_c, st) + mn[:, :PAIR]


def _rwkv_scan(p_all, params, tables):
    batch, n_tok, _ = p_all.shape
    n_blk = n_tok // TOK_BLOCK
    n_halo = TOK_BLOCK // HALO
    head0 = COL_HEADS // D_RWKV
    lora0 = COL_LORA // PAIR
    blk_of = _scan_block(n_blk)

    def main(col):
        return pl.BlockSpec((1, TOK_BLOCK, D_RWKV), lambda b, d, j: (b, blk_of(d, j), col))

    def prev(col, width=D_RWKV, col_of=None):
        return pl.BlockSpec(
            (1, HALO, width),
            lambda b, d, j: (b, jnp.maximum(blk_of(d, j) * n_halo - 1, 0),
                             col if col_of is None else col_of(d)))

    def nxt(col, width=D_RWKV, col_of=None):
        return pl.BlockSpec(
            (1, HALO, width),
            lambda b, d, j: (b, jnp.minimum((blk_of(d, j) + 1) * n_halo, n_tok // HALO - 1),
                             col if col_of is None else col_of(d)))

    lora_col = lambda d: lora0 + d
    full = lambda shape: pl.BlockSpec(shape, lambda b, d, j: (0,) * len(shape))
    per_dir = lambda shape: pl.BlockSpec((1,) + shape, lambda b, d, j: (d,) + (0,) * len(shape))
    out_spec = pl.BlockSpec((1, 1, TOK_BLOCK, D_RWKV), lambda b, d, j: (d, b, blk_of(d, j), 0))

    rw_masks, _, cums, _ = tables
    return pl.pallas_call(
        functools.partial(_rwkv_kernel, n_blk=n_blk),
        grid=(batch, 2, n_blk),
        in_specs=[
            main(head0), main(head0 + 1), main(head0 + 2),
            pl.BlockSpec((1, TOK_BLOCK, PAIR), lambda b, d, j: (b, blk_of(d, j), lora0 + d)),
            prev(head0), prev(head0 + 1), prev(head0 + 2), prev(0, PAIR, lora_col),
            nxt(head0), nxt(head0 + 1), nxt(head0 + 2), nxt(0, PAIR, lora_col),
            full((1, 3, D_RWKV)), full((1, 3, D_RWKV)), full((1, 3, D_RWKV)), per_dir((3, PAIR)),
            per_dir((1, D_RWKV)), per_dir((PAIR, D_RWKV)), per_dir((1, D_RWKV)), per_dir((PAIR, D_RWKV)),
            full((1, D_RWKV)), full((1, D_RWKV)), per_dir((1, D_RWKV)),
            full((D_RWKV, D_RWKV)), per_dir(rw_masks.shape[1:]), per_dir((CHUNK, CHUNK)),
        ],
        out_specs=[out_spec, out_spec],
        out_shape=[jax.ShapeDtypeStruct((2, batch, n_tok, D_RWKV), _F32)] * 2,
        scratch_shapes=[pltpu.VMEM((6, TOK_BLOCK, D_RWKV), _F32),
                        pltpu.VMEM((N_PAIR, PAIR, PAIR), _F32)],
        compiler_params=pltpu.CompilerParams(
            dimension_semantics=("parallel", "arbitrary", "arbitrary"),
            vmem_limit_bytes=VMEM_LIMIT_BYTES),
        name="rwkv_scan",
    )(p_all, p_all, p_all, p_all, p_all, p_all, p_all, p_all, p_all, p_all, p_all, p_all,
      params["shift_r"], params["shift_k"], params["shift_v"], params["shift_lora"],
      params["w0"], params["w_up"], params["a0"], params["a_up"],
      params["k_k"], params["k_a"], params["r_k"],
      _head_mean_table(), rw_masks, cums)


def _hgrn_kernel(q_ref, i_ref, z_ref, lb_ref, mask_ref, sum_ref, o_ref, tok_ref, state_ref, *, layer):
    d = pl.program_id(1)
    j = pl.program_id(2)

    @pl.when(j == 0)
    def _():
        state_ref[...] = jnp.zeros_like(state_ref)

    logits = lb_ref[0]
    e = jnp.exp(logits - jnp.max(logits, axis=0, keepdims=True))
    lb_w = e / jnp.sum(e, axis=0, keepdims=True)
    csum = lb_w[0:1]
    for l in range(1, layer + 1):
        csum = csum + lb_w[l:l + 1]
    lb = jnp.maximum(csum - lb_w[0:1], 0.0)

    z = z_ref[0]
    log_lb = jnp.log(lb)
    log_rest = jnp.log1p(-lb) - _softplus(-z)
    tok_ref[0] = jnp.maximum(log_lb, log_rest) + jnp.log1p(jnp.exp(-jnp.abs(log_lb - log_rest)))
    tok_ref[1] = (1.0 - lb) * jax.nn.sigmoid(-z)

    lane_lo, lane_hi = _lane_masks()
    n_lv = len(CHUNK_LEVELS)

    for c in range(N_CHUNK):
        start = pl.multiple_of(jnp.where(d == 0, c, N_CHUNK - 1 - c) * CHUNK, CHUNK)
        rows = pl.ds(start, CHUNK)
        sums = _dot(sum_ref[0], tok_ref[0, rows, :])
        g_incl = sums[:CHUNK]
        g_after = sums[CHUNK:2 * CHUNK]
        p_end = jnp.exp(g_incl[0:1] + g_after[0:1])
        q_c = q_ref[0, rows, :]
        k_c = tok_ref[1, rows, :]
        i_c = i_ref[0, rows, :]
        qd = q_c * jnp.exp(g_incl)
        kd = k_c * jnp.exp(g_after)
        lv_q, lv_k = [], []
        for lv in range(n_lv):
            x = jnp.exp(sums[(2 + lv) * CHUNK:(3 + lv) * CHUNK])
            lv_q.append(q_c * x)
            lv_k.append(k_c * x)

        for p in range(N_PAIR):
            sl = slice(p * PAIR, (p + 1) * PAIR)
            stack = lambda x: _stack_heads(x[:, sl], lane_lo, lane_hi)
            scores = _dot_nt(stack(q_c), stack(k_c)) * mask_ref[0, n_lv]
            for lv in range(n_lv):
                scores = scores + _dot_nt(stack(lv_q[lv]), stack(lv_k[lv])) * mask_ref[0, lv]
            i2 = stack(i_c)
            st = state_ref[p]
            o2 = _dot(scores, i2) + _dot_nt(stack(qd), st)
            o_ref[0, 0, rows, sl] = _unstack_heads(o2)
            state_ref[p] = st * p_end[:, sl] + _dot_tn(i2, stack(kd))


def _hgrn_scan(p_all, lb_logits, tables, layer):
    batch, n_tok, _ = p_all.shape
    n_blk = n_tok // TOK_BLOCK
    head0 = COL_HEADS // D_HGRN
    blk_of = _scan_block(n_blk)
    _, hg_masks, _, hg_sums = tables
    per_dir = lambda shape: pl.BlockSpec((1,) + shape, lambda b, d, j: (d,) + (0,) * len(shape))
    return pl.pallas_call(
        functools.partial(_hgrn_kernel, layer=layer),
        grid=(batch, 2, n_blk),
        in_specs=[
            pl.BlockSpec((1, TOK_BLOCK, D_HGRN), lambda b, d, j: (b, blk_of(d, j), head0 + 3)),
            pl.BlockSpec((1, TOK_BLOCK, D_HGRN), lambda b, d, j: (b, blk_of(d, j), head0 + 4)),
            pl.BlockSpec((1, TOK_BLOCK, D_HGRN), lambda b, d, j: (b, blk_of(d, j), head0 + 5 + d)),
            per_dir(lb_logits.shape[1:]), per_dir(hg_masks.shape[1:]), per_dir(hg_sums.shape[1:]),
        ],
        out_specs=pl.BlockSpec((1, 1, TOK_BLOCK, D_HGRN), lambda b, d, j: (d, b, blk_of(d, j), 0)),
        out_shape=jax.ShapeDtypeStruct((2, batch, n_tok, D_HGRN), _F32),
        scratch_shapes=[pltpu.VMEM((2, TOK_BLOCK, D_HGRN), _F32),
                        pltpu.VMEM((N_PAIR, PAIR, PAIR), _F32)],
        compiler_params=pltpu.CompilerParams(
            dimension_semantics=("parallel", "arbitrary", "arbitrary"),
            vmem_limit_bytes=VMEM_LIMIT_BYTES),
        name="hgrn_scan",
    )(p_all, p_all, p_all, lb_logits, hg_masks, hg_sums)


def _readout_kernel(h_ref, gate_ref, pool_ref, yf_ref, yb_ref, bf_ref, bb_ref, of_ref, ob_ref,
                    gt_ref, pm_ref, pw_ref, ps_ref, gng_ref, gnb_ref, ng_ref, hm_ref,
                    wout_ref, lng_ref, lnb_ref, out_ref, *, alpha):
    pv = pool_ref[0]
    group = lax.broadcasted_iota(jnp.int32, pv.shape, 1) // POOL_GROUP
    pooled = jnp.zeros_like(pv)
    for g in range(len(POOL_WINDOWS)):
        pooled = pooled + jnp.where(group == g, _dot(pm_ref[0, g], pv), 0.0)
    mixed = _dot(pooled, pw_ref[...]) * ps_ref[...]

    y = yf_ref[0, 0] + yb_ref[0, 0]
    yc = y - _dot(y, hm_ref[...])
    var = _dot(yc * yc, hm_ref[...])
    rw = yc * lax.rsqrt(var + GN_EPS) * gng_ref[...] + gnb_ref[...] + bf_ref[0, 0] + bb_ref[0, 0]

    o = of_ref[0, 0] + ob_ref[0, 0]
    ho = o * lax.rsqrt(_dot(o * o, hm_ref[...]) + RMS_EPS) * ng_ref[...]

    gate = gate_ref[0]
    mix = jnp.concatenate([mixed, rw, ho], axis=1) * (gate * jax.nn.sigmoid(gate))
    proj = jnp.dot(mix.astype(jnp.bfloat16), wout_ref[...], preferred_element_type=_F32)
    out_ref[0] = _layer_norm(alpha * h_ref[0] + gt_ref[0] * proj) * lng_ref[...] + lnb_ref[...]


def _readout(h, p_all, y, bonus, o, gate_mod, params, alpha, skip_ctx):
    batch, n_tok, _ = h.shape
    n_blk = n_tok // TOK_BLOCK
    off = 1 if skip_ctx else 0
    tok = lambda width, col: pl.BlockSpec((1, TOK_BLOCK, width), lambda b, j: (b, j + off, col))
    dirs = lambda d: pl.BlockSpec((1, 1, TOK_BLOCK, D_RWKV), lambda b, j: (d, b, j + off, 0))
    full = lambda shape: pl.BlockSpec(shape, lambda b, j: (0,) * len(shape))
    n_win = len(POOL_WINDOWS)
    return pl.pallas_call(
        functools.partial(_readout_kernel, alpha=alpha),
        grid=(batch, n_blk - off),
        in_specs=[
            tok(D_MODEL, 0), tok(D_MODEL, COL_GATE // D_MODEL), tok(D_POOL, COL_POOL // D_POOL),
            dirs(0), dirs(1), dirs(0), dirs(1), dirs(0), dirs(1),
            pl.BlockSpec((1, 1, D_MODEL), lambda b, j: (jnp.where(j + off == 0, batch, b), 0, 0)),
            pl.BlockSpec((1, n_win, TOK_BLOCK, TOK_BLOCK),
                         lambda b, j: (jnp.where(j + off == 0, 0, 1), 0, 0, 0)),
            full((D_POOL, D_POOL)), full((1, D_POOL)),
            full((1, D_RWKV)), full((1, D_RWKV)), full((1, D_HGRN)), full((D_RWKV, D_RWKV)),
            full((D_MODEL, D_MODEL)), full((1, D_MODEL)), full((1, D_MODEL)),
        ],
        out_specs=pl.BlockSpec((1, TOK_BLOCK, D_MODEL), lambda b, j: (b, j, 0)),
        out_shape=jax.ShapeDtypeStruct((batch, n_tok - off * TOK_BLOCK, D_MODEL), _F32),
        compiler_params=pltpu.CompilerParams(
            dimension_semantics=("parallel", "parallel"), vmem_limit_bytes=VMEM_LIMIT_BYTES),
        name="readout",
    )(h, p_all, p_all, y, y, bonus, bonus, o, o, gate_mod,
      jnp.asarray(_pool_tables()), params["pool_w"], params["pool_scale"],
      params["gn_g"], params["gn_b"], params["norm_g"], _head_mean_table(),
      params["w_out"], params["ln_g"], params["ln_b"])


def _block_diag(blocks):
    n, a, b = blocks.shape
    out = jnp.zeros((n * a, n * b), blocks.dtype)
    for g in range(n):
        out = out.at[g * a:(g + 1) * a, g * b:(g + 1) * b].set(blocks[g])
    return out


def _layer_params(l, w_in, rwkv_shift, pool_w, pool_scale, rwkv_w0, rwkv_w_up, rwkv_a0, rwkv_a_up,
                  rwkv_k_k, rwkv_k_a, rwkv_r_k, rwkv_gn_g, rwkv_gn_b, hgrn_norm_g, w_out, ln_g, ln_b):
    sh = rwkv_shift[l]
    lora_sh = jnp.stack([
        jnp.concatenate([sh[:, 1152 + d * LORA:1152 + (d + 1) * LORA],
                         sh[:, 1280 + d * LORA:1280 + (d + 1) * LORA]], axis=1) for d in range(2)])
    pad = jnp.zeros((2, LORA, D_RWKV), _F32)
    return {
        "w_in": w_in[l][:, _column_perm()].astype(jnp.bfloat16),
        "shift_r": sh[None, :, 0:384], "shift_k": sh[None, :, 384:768], "shift_v": sh[None, :, 768:1152],
        "shift_lora": lora_sh,
        "w0": rwkv_w0[l][:, None, :], "a0": rwkv_a0[l][:, None, :],
        "w_up": jnp.concatenate([rwkv_w_up[l], pad], axis=1),
        "a_up": jnp.concatenate([pad, rwkv_a_up[l]], axis=1),
        "k_k": rwkv_k_k[l][None], "k_a": rwkv_k_a[l][None],
        "r_k": rwkv_r_k[l].reshape(2, 1, D_RWKV),
        "pool_w": _block_diag(pool_w[l]), "pool_scale": pool_scale[l][None],
        "gn_g": rwkv_gn_g[l][None], "gn_b": rwkv_gn_b[l][None], "norm_g": hgrn_norm_g[l][None],
        "w_out": w_out[l].astype(jnp.bfloat16), "ln_g": ln_g[l][None], "ln_b": ln_b[l][None],
    }


def kernel(x, c, ctx, c_ctx, mod_w, mod_b, w_in, rwkv_shift, pool_w, pool_scale, rwkv_w0, rwkv_w_up,
           rwkv_a0, rwkv_a_up, rwkv_k_k, rwkv_k_a, rwkv_r_k, rwkv_gn_g, rwkv_gn_b, hgrn_lb_logits,
           hgrn_norm_g, w_out, ln_g, ln_b):
    depth = mod_w.shape[0]
    batch = x.shape[0]
    assert ctx.shape[1] == CTX_LEN == TOK_BLOCK and x.shape[1] % TOK_BLOCK == 0
    alpha = float((2 * depth) ** 0.25)
    tables = tuple(jnp.asarray(t) for t in _scan_tables())

    c_all = jnp.concatenate([c.astype(_F32), c_ctx.astype(_F32)[None]], axis=0)
    mod = _modulation(c_all, mod_w, mod_b)
    h = jnp.concatenate([ctx.astype(_F32), x.astype(_F32)], axis=1)

    for l in range(depth):
        prm = _layer_params(l, w_in, rwkv_shift, pool_w, pool_scale, rwkv_w0, rwkv_w_up, rwkv_a0,
                            rwkv_a_up, rwkv_k_k, rwkv_k_a, rwkv_r_k, rwkv_gn_g, rwkv_gn_b,
                            hgrn_norm_g, w_out, ln_g, ln_b)
        shift = mod[l, :, None, 0:D_MODEL]
        scale = mod[l, :, None, D_MODEL:2 * D_MODEL]
        gate_mod = mod[l, :, None, 2 * D_MODEL:]
        p_all = _input_projection(h, shift, scale, prm["w_in"])
        y, bonus = _rwkv_scan(p_all, prm, tables)
        o = _hgrn_scan(p_all, hgrn_lb_logits, tables, l)
        h = _readout(h, p_all, y, bonus, o, gate_mod, prm, alpha, skip_ctx=(l == depth - 1))
    return h.astype(x.dtype)
```

```python
import functools

import numpy as np
import jax
import jax.numpy as jnp
from jax import lax
from jax.experimental import pallas as pl
from jax.experimental.pallas import tpu as pltpu

D_MODEL = 1024
CTX_LEN = 256
GRID_W = 64
HEAD_DIM = 64
D_POOL = 256
D_RWKV = 384
D_HGRN = 384
POOL_WINDOWS = (2, 4, 8, 16)
POOL_GROUP = 64
LORA = 64
D_IN = 4224
LN_EPS = 1e-5
GN_EPS = 64e-5
RMS_EPS = 1e-6

TOK_BLOCK = 256
CHUNK = 64
PAIR = 2 * HEAD_DIM
N_PAIR = D_RWKV // PAIR
N_CHUNK = TOK_BLOCK // CHUNK
CHUNK_LEVELS = (1, 2, 4, 8, 16, 32)
HALO = 8
VMEM_LIMIT_BYTES = 56 * 1024 * 1024

COL_GATE = 0
COL_POOL = 1024
COL_LORA = 1280
COL_HEADS = 1536

_F32 = jnp.float32
_HI = lax.Precision.HIGHEST


def _softplus(x):
    return jnp.maximum(x, 0.0) + jnp.log1p(jnp.exp(-jnp.abs(x)))


def _dot_exact(a, b):
    return jnp.dot(a, b, precision=_HI, preferred_element_type=_F32)


def _bf16(x):
    return x.astype(jnp.bfloat16)


def _dot(a, b):
    return jnp.dot(_bf16(a), _bf16(b), preferred_element_type=_F32)


def _dot_split(table, x):
    head = _bf16(x)
    rest = _bf16(x - head.astype(_F32))
    t = _bf16(table)
    return (jnp.dot(t, head, preferred_element_type=_F32) + jnp.dot(t, rest, preferred_element_type=_F32))


def _dot_nt(a, b):
    return lax.dot_general(_bf16(a), _bf16(b), (((1,), (1,)), ((), ())), preferred_element_type=_F32)


def _dot_tn(a, b):
    return lax.dot_general(_bf16(a), _bf16(b), (((0,), (0,)), ((), ())), preferred_element_type=_F32)


def _chunk_pos(direction):
    t = np.arange(CHUNK)
    return t if direction == 0 else CHUNK - 1 - t


def _level_tables(direction):
    pos = _chunk_pos(direction)
    masks, sums = [], []
    for m in CHUNK_LEVELS:
        blk = pos // m
        pair = pos // (2 * m)
        late = (blk % 2) == 1
        same = pair[:, None] == pair[None, :]
        masks.append(same & late[:, None] & (~late)[None, :])
        late_rows = same & late[:, None] & late[None, :] & (pos[None, :] <= pos[:, None])
        early_rows = same & (~late)[:, None] & (~late)[None, :] & (pos[None, :] > pos[:, None])
        sums.append(late_rows | early_rows)
    return masks, sums


def _tile2(m):
    return np.kron(np.eye(2, dtype=np.float32), m.astype(np.float32))


@functools.lru_cache(maxsize=None)
def _scan_tables():
    rw_masks, hg_masks, cums, hg_sums = [], [], [], []
    for d in range(2):
        pos = _chunk_pos(d)
        strict = pos[None, :] < pos[:, None]
        incl = pos[None, :] <= pos[:, None]
        after = pos[None, :] > pos[:, None]
        lv_masks, lv_sums = _level_tables(d)
        eye = np.eye(CHUNK, dtype=bool)
        rw_masks.append(np.stack([_tile2(strict), _tile2(incl)] + [_tile2(m) for m in lv_masks]
                                 + [np.eye(PAIR, dtype=np.float32)]))
        hg_masks.append(np.stack([_tile2(m) for m in lv_masks] + [_tile2(eye)]
                                 + [_tile2(np.ones((HEAD_DIM, HEAD_DIM)))]))
        cums.append(incl.astype(np.float32))
        hg_sums.append(np.concatenate([incl, after] + lv_sums, axis=0).astype(np.float32))
    return (np.stack(rw_masks), np.stack(hg_masks), np.stack(cums), np.stack(hg_sums))


@functools.lru_cache(maxsize=None)
def _pool_tables():
    out = np.zeros((2, len(POOL_WINDOWS), TOK_BLOCK, TOK_BLOCK), np.float32)
    for kind, seg in enumerate((TOK_BLOCK, GRID_W)):
        for g, win in enumerate(POOL_WINDOWS):
            left = win // 2
            right = win - 1 - left
            for t in range(TOK_BLOCK):
                base = (t // seg) * seg
                lo = max(t - left, base)
                hi = min(t + right, base + seg - 1) + 1
                out[kind, g, t, lo:hi] = 1.0 / (hi - lo)
                out[kind, g, t, t] -= 1.0
    return out


@functools.lru_cache(maxsize=None)
def _head_mean_table():
    h = np.arange(D_RWKV) // HEAD_DIM
    return (h[:, None] == h[None, :]).astype(np.float32) / HEAD_DIM


@functools.lru_cache(maxsize=None)
def _column_perm():
    pool = np.arange(0, 256)
    u0 = 256
    r = u0 + np.arange(0, 384)
    k = u0 + np.arange(384, 768)
    v = u0 + np.arange(768, 1152)
    lora = []
    for d in range(2):
        lora.append(u0 + 1152 + d * LORA + np.arange(LORA))
        lora.append(u0 + 1280 + d * LORA + np.arange(LORA))
    f0 = 256 + 1408
    q = f0 + np.arange(0, 384)
    i = f0 + np.arange(384, 768)
    z0 = f0 + np.arange(768, 1152)
    z1 = f0 + np.arange(1152, 1536)
    gate = np.arange(3200, 4224)
    perm = np.concatenate([gate, pool] + lora + [r, k, v, q, i, z0, z1])
    assert perm.shape[0] == D_IN and np.unique(perm).shape[0] == D_IN
    return perm


def _mod_kernel(c_ref, w_ref, b_ref, o_ref):
    c = c_ref[...]
    o_ref[0] = _dot_exact(c * jax.nn.sigmoid(c), w_ref[0]) + b_ref[0]


def _modulation(c_all, mod_w, mod_b):
    depth = mod_w.shape[0]
    rows = c_all.shape[0]
    n_tile = 1024
    return pl.pallas_call(
        _mod_kernel,
        grid=(depth, 3 * D_MODEL // n_tile),
        in_specs=[
            pl.BlockSpec((rows, D_MODEL), lambda l, n: (0, 0)),
            pl.BlockSpec((1, D_MODEL, n_tile), lambda l, n: (l, 0, n)),
            pl.BlockSpec((1, 1, n_tile), lambda l, n: (l, 0, n)),
        ],
        out_specs=pl.BlockSpec((1, rows, n_tile), lambda l, n: (l, 0, n)),
        out_shape=jax.ShapeDtypeStruct((depth, rows, 3 * D_MODEL), _F32),
        compiler_params=pltpu.CompilerParams(
            dimension_semantics=("arbitrary", "arbitrary"), vmem_limit_bytes=VMEM_LIMIT_BYTES),
        name="modulation",
    )(c_all, mod_w, mod_b.reshape(depth, 1, 3 * D_MODEL))


def _layer_norm(x):
    mu = jnp.mean(x, axis=-1, keepdims=True)
    xc = x - mu
    var = jnp.mean(xc * xc, axis=-1, keepdims=True)
    return xc * lax.rsqrt(var + LN_EPS)


def _proj_kernel(h_ref, sh_ref, sc_ref, w_ref, p_ref):
    xn = _layer_norm(h_ref[0]) * (1.0 + sc_ref[0]) + sh_ref[0]
    p_ref[0] = jnp.dot(xn.astype(jnp.bfloat16), w_ref[...], preferred_element_type=_F32)


def _mod_row(batch):
    return lambda b, j: (jnp.where(j == 0, batch, b), 0, 0)


def _input_projection(h, shift, scale, w_in_bf16):
    batch, n_tok, _ = h.shape
    return pl.pallas_call(
        _proj_kernel,
        grid=(batch, n_tok // TOK_BLOCK),
        in_specs=[
            pl.BlockSpec((1, TOK_BLOCK, D_MODEL), lambda b, j: (b, j, 0)),
            pl.BlockSpec((1, 1, D_MODEL), _mod_row(batch)),
            pl.BlockSpec((1, 1, D_MODEL), _mod_row(batch)),
            pl.BlockSpec((D_MODEL, D_IN), lambda b, j: (0, 0)),
        ],
        out_specs=pl.BlockSpec((1, TOK_BLOCK, D_IN), lambda b, j: (b, j, 0)),
        out_shape=jax.ShapeDtypeStruct((batch, n_tok, D_IN), _F32),
        compiler_params=pltpu.CompilerParams(
            dimension_semantics=("parallel", "parallel"), vmem_limit_bytes=VMEM_LIMIT_BYTES),
        name="input_projection",
    )(h, shift, scale, w_in_bf16)


def _scan_block(n_blk):
    return lambda d, j: jnp.where((d == 0) | (j == 0), j, n_blk - j)


def _stack_heads(x, lane_lo, lane_hi):
    return jnp.concatenate([x * lane_lo, x * lane_hi], axis=0)


def _unstack_heads(x2):
    return x2[:CHUNK] + x2[CHUNK:]


def _lane_masks():
    lane = lax.broadcasted_iota(jnp.int32, (CHUNK, PAIR), 1)
    lo = (lane < HEAD_DIM).astype(_F32)
    return lo, 1.0 - lo


def _rwkv_kernel(r_ref, k_ref, v_ref, lo_ref,
                 rp_ref, kp_ref, vp_ref, lp_ref, rn_ref, kn_ref, vn_ref, ln_ref,
                 shr_ref, shk_ref, shv_ref, shl_ref,
                 w0_ref, wup_ref, a0_ref, aup_ref, kk_ref, ka_ref, rk_ref,
                 hm_ref, mask_ref, cum_ref,
                 y_ref, bonus_ref,
                 tok_ref, state_ref, *, n_blk):
    d = pl.program_id(1)
    j = pl.program_id(2)
    blk = _scan_block(n_blk)(d, j)
    prev_ok = jnp.where((blk == 0) | (blk == 1), 0.0, 1.0).astype(_F32)
    next_ok = jnp.where((blk == 0) | (blk == n_blk - 1), 0.0, 1.0).astype(_F32)

    @pl.when(j == 0)
    def _():
        state_ref[...] = jnp.zeros_like(state_ref)

    def shifted(main_ref, prev_ref, next_ref, w_ref):
        x = main_ref[0]
        w = w_ref[0]
        row = lax.broadcasted_iota(jnp.int32, x.shape, 0)
        prev_row = prev_ref[0, HALO - 1:HALO, :] * prev_ok
        next_row = next_ref[0, 0:1, :] * next_ok
        x_prev = jnp.where(row == 0, prev_row, pltpu.roll(x, 1, axis=0))
        x_next = jnp.where(row == TOK_BLOCK - 1, next_row, pltpu.roll(x, TOK_BLOCK - 1, axis=0))
        return w[0:1] * x_prev + w[1:2] * x + w[2:3] * x_next

    r = shifted(r_ref, rp_ref, rn_ref, shr_ref)
    k = shifted(k_ref, kp_ref, kn_ref, shk_ref)
    v = shifted(v_ref, vp_ref, vn_ref, shv_ref)
    lora = shifted(lo_ref, lp_ref, ln_ref, shl_ref)

    w_pre = w0_ref[0] + _dot(jnp.tanh(lora), wup_ref[0])
    w_log = -_softplus(-w_pre) - 0.5
    lw = -jnp.exp(w_log)
    a = jax.nn.sigmoid(a0_ref[0] + _dot(lora, aup_ref[0]))
    kk = k * kk_ref[...]
    kk = kk * lax.rsqrt(_dot(kk * kk, hm_ref[...]) * HEAD_DIM + 1e-12)
    k_mod = k * (1.0 + (a - 1.0) * ka_ref[...])
    bonus_ref[0, 0] = _dot(r * k_mod * rk_ref[0], hm_ref[...]) * HEAD_DIM * v

    tok_ref[0] = lw
    tok_ref[1] = kk
    tok_ref[2] = a * kk
    tok_ref[3] = k_mod
    tok_ref[4] = v
    tok_ref[5] = r

    lane_lo, lane_hi = _lane_masks()
    m_strict = mask_ref[0, 0]
    m_incl = mask_ref[0, 1]
    eye = mask_ref[0, 2 + len(CHUNK_LEVELS)]
    zeros = jnp.zeros((PAIR, PAIR), _F32)

    chunks = []
    for c in range(N_CHUNK):
        start = pl.multiple_of(jnp.where(d == 0, c, N_CHUNK - 1 - c) * CHUNK, CHUNK)
        rows = pl.ds(start, CHUNK)
        lw_c = tok_ref[0, rows, :]
        g_incl = _dot_split(cum_ref[0], lw_c)
        g_excl = g_incl - lw_c
        g_mid = g_incl[CHUNK // 2:CHUNK // 2 + 1, :]
        g_end = jnp.sum(lw_c, axis=0, keepdims=True)
        e_in = jnp.exp(g_incl - g_mid)
        e_out = jnp.exp(g_mid - g_incl)
        e_end = jnp.exp(g_end - g_incl)
        kk_c = tok_ref[1, rows, :]
        b_c = tok_ref[2, rows, :]
        k_c = tok_ref[3, rows, :]
        chunks.append(dict(
            rows=rows, p_end=jnp.exp(g_end),
            p_mid=jnp.exp(g_mid),
            kkd=kk_c * jnp.exp(g_excl - g_mid), rd=tok_ref[5, rows, :] * e_in,
            ki=k_c * e_out, bi=b_c * e_out, kd=k_c * e_end, bd=b_c * e_end, v=tok_ref[4, rows, :]))

    inst = [(c, p) for c in range(N_CHUNK) for p in range(N_PAIR)]
    lanes = lambda p: slice(p * PAIR, (p + 1) * PAIR)
    nat = lambda c, p, name: chunks[c][name][:, lanes(p)]
    stk = lambda c, p, name: _stack_heads(nat(c, p, name), lane_lo, lane_hi)

    a_k, a_b, r_k, r_b = {}, {}, {}, {}
    for c, p in inst:
        ki, bi = nat(c, p, "ki"), nat(c, p, "bi")
        gm = _dot_nt(jnp.concatenate([stk(c, p, "kkd"), stk(c, p, "rd")], axis=0),
                     jnp.concatenate([ki, ki, bi, bi], axis=0))
        a_k[c, p] = gm[:PAIR, :PAIR] * m_strict
        a_b[c, p] = gm[:PAIR, PAIR:] * m_strict
        r_k[c, p] = gm[PAIR:, :PAIR] * m_incl
        r_b[c, p] = gm[PAIR:, PAIR:] * m_incl

    t_inv = {cp: eye - a_b[cp] * mask_ref[0, 2] for cp in inst}
    w0 = {(c, p): _dot(a_k[c, p], stk(c, p, "v")) for c, p in inst}
    for lv in range(1, len(CHUNK_LEVELS)):
        x = {cp: _dot(t_inv[cp], a_b[cp] * mask_ref[0, 2 + lv]) for cp in inst}
        t_inv = {cp: t_inv[cp] - _dot(x[cp], t_inv[cp]) for cp in inst}

    yr, mn = {}, {}
    for c, p in inst:
        kkd_abs = stk(c, p, "kkd") * chunks[c]["p_mid"][:, lanes(p)]
        twk = _dot(t_inv[c, p], jnp.concatenate([w0[c, p], kkd_abs], axis=1))
        z = jnp.concatenate([jnp.concatenate([stk(c, p, "v"), zeros], axis=1), twk], axis=0)
        yr[c, p] = _dot(jnp.concatenate([r_k[c, p], -r_b[c, p]], axis=1), z)
        mn[c, p] = _dot_tn(jnp.concatenate([stk(c, p, "kd"), -stk(c, p, "bd")], axis=0), z)

    for c in range(N_CHUNK):
        for p in range(N_PAIR):
            st = state_ref[p]
            rq = stk(c, p, "rd") * chunks[c]["p_mid"][:, lanes(p)] + yr[c, p][:, PAIR:]
            y2 = yr[c, p][:, :PAIR] + _dot(rq, st)
            y_ref[0, 0, chunks[c]["rows"], lanes(p)] = _unstack_heads(y2)
            m_c = mn[c, p][:, PAIR:] + eye * chunks[c]["p_end"][:, lanes(p)]
            state_ref[p] = _dot(m_c, st) + mn[c, p][:, :PAIR]


def _rwkv_scan(p_all, params, tables):
    batch, n_tok, _ = p_all.shape
    n_blk = n_tok // TOK_BLOCK
    n_halo = TOK_BLOCK // HALO
    head0 = COL_HEADS // D_RWKV
    lora0 = COL_LORA // PAIR
    blk_of = _scan_block(n_blk)

    def main(col):
        return pl.BlockSpec((1, TOK_BLOCK, D_RWKV), lambda b, d, j: (b, blk_of(d, j), col))

    def prev(col, width=D_RWKV, col_of=None):
        return pl.BlockSpec(
            (1, HALO, width),
            lambda b, d, j: (b, jnp.maximum(blk_of(d, j) * n_halo - 1, 0),
                             col if col_of is None else col_of(d)))

    def nxt(col, width=D_RWKV, col_of=None):
        return pl.BlockSpec(
            (1, HALO, width),
            lambda b, d, j: (b, jnp.minimum((blk_of(d, j) + 1) * n_halo, n_tok // HALO - 1),
                             col if col_of is None else col_of(d)))

    lora_col = lambda d: lora0 + d
    full = lambda shape: pl.BlockSpec(shape, lambda b, d, j: (0,) * len(shape))
    per_dir = lambda shape: pl.BlockSpec((1,) + shape, lambda b, d, j: (d,) + (0,) * len(shape))
    out_spec = pl.BlockSpec((1, 1, TOK_BLOCK, D_RWKV), lambda b, d, j: (d, b, blk_of(d, j), 0))

    rw_masks, _, cums, _ = tables
    return pl.pallas_call(
        functools.partial(_rwkv_kernel, n_blk=n_blk),
        grid=(batch, 2, n_blk),
        in_specs=[
            main(head0), main(head0 + 1), main(head0 + 2),
            pl.BlockSpec((1, TOK_BLOCK, PAIR), lambda b, d, j: (b, blk_of(d, j), lora0 + d)),
            prev(head0), prev(head0 + 1), prev(head0 + 2), prev(0, PAIR, lora_col),
            nxt(head0), nxt(head0 + 1), nxt(head0 + 2), nxt(0, PAIR, lora_col),
            full((1, 3, D_RWKV)), full((1, 3, D_RWKV)), full((1, 3, D_RWKV)), per_dir((3, PAIR)),
            per_dir((1, D_RWKV)), per_dir((PAIR, D_RWKV)), per_dir((1, D_RWKV)), per_dir((PAIR, D_RWKV)),
            full((1, D_RWKV)), full((1, D_RWKV)), per_dir((1, D_RWKV)),
            full((D_RWKV, D_RWKV)), per_dir(rw_masks.shape[1:]), per_dir((CHUNK, CHUNK)),
        ],
        out_specs=[out_spec, out_spec],
        out_shape=[jax.ShapeDtypeStruct((2, batch, n_tok, D_RWKV), _F32)] * 2,
        scratch_shapes=[pltpu.VMEM((6, TOK_BLOCK, D_RWKV), _F32),
                        pltpu.VMEM((N_PAIR, PAIR, PAIR), _F32)],
        compiler_params=pltpu.CompilerParams(
            dimension_semantics=("parallel", "arbitrary", "arbitrary"),
            vmem_limit_bytes=VMEM_LIMIT_BYTES),
        name="rwkv_scan",
    )(p_all, p_all, p_all, p_all, p_all, p_all, p_all, p_all, p_all, p_all, p_all, p_all,
      params["shift_r"], params["shift_k"], params["shift_v"], params["shift_lora"],
      params["w0"], params["w_up"], params["a0"], params["a_up"],
      params["k_k"], params["k_a"], params["r_k"],
      _head_mean_table(), rw_masks, cums)


def _hgrn_kernel(q_ref, i_ref, z_ref, lb_ref, mask_ref, sum_ref, o_ref, tok_ref, state_ref, *, layer):
    d = pl.program_id(1)
    j = pl.program_id(2)

    @pl.when(j == 0)
    def _():
        state_ref[...] = jnp.zeros_like(state_ref)

    logits = lb_ref[0]
    e = jnp.exp(logits - jnp.max(logits, axis=0, keepdims=True))
    lb_w = e / jnp.sum(e, axis=0, keepdims=True)
    csum = lb_w[0:1]
    for l in range(1, layer + 1):
        csum = csum + lb_w[l:l + 1]
    lb = jnp.maximum(csum - lb_w[0:1], 0.0)

    z = z_ref[0]
    log_lb = jnp.log(lb)
    log_rest = jnp.log1p(-lb) - _softplus(-z)
    tok_ref[0] = jnp.maximum(log_lb, log_rest) + jnp.log1p(jnp.exp(-jnp.abs(log_lb - log_rest)))
    tok_ref[1] = (1.0 - lb) * jax.nn.sigmoid(-z)

    lane_lo, lane_hi = _lane_masks()
    n_lv = len(CHUNK_LEVELS)

    head_blocks = mask_ref[0, n_lv + 1]
    lanes = lambda p: slice(p * PAIR, (p + 1) * PAIR)
    twice = lambda x: jnp.concatenate([x, x], axis=0)

    chunks, intra = [], {}
    for c in range(N_CHUNK):
        start = pl.multiple_of(jnp.where(d == 0, c, N_CHUNK - 1 - c) * CHUNK, CHUNK)
        rows = pl.ds(start, CHUNK)
        sums = _dot_split(sum_ref[0], tok_ref[0, rows, :])
        g_incl = sums[:CHUNK]
        g_after = sums[CHUNK:2 * CHUNK]
        q_c = q_ref[0, rows, :]
        k_c = tok_ref[1, rows, :]
        i_c = i_ref[0, rows, :]
        chunks.append(dict(rows=rows, p_end=jnp.exp(g_incl[0:1] + g_after[0:1]),
                           qd=q_c * jnp.exp(g_incl), kd=k_c * jnp.exp(g_after), i=i_c))
        lv_q, lv_k = [q_c], [k_c]
        for lv in range(n_lv):
            x = jnp.exp(sums[(2 + lv) * CHUNK:(3 + lv) * CHUNK])
            lv_q.append(q_c * x)
            lv_k.append(k_c * x)
        lv_mask = [mask_ref[0, n_lv]] + [mask_ref[0, lv] for lv in range(n_lv)]
        for p in range(N_PAIR):
            scores = None
            for q_l, k_l, m_l in zip(lv_q, lv_k, lv_mask):
                s_l = _dot_nt(_stack_heads(q_l[:, lanes(p)], lane_lo, lane_hi), twice(k_l[:, lanes(p)])) * m_l
                scores = s_l if scores is None else scores + s_l
            o2 = _dot(scores, twice(i_c[:, lanes(p)]))
            intra[c, p] = jnp.where(lane_lo > 0.5, o2[:CHUNK], o2[CHUNK:])

    for c in range(N_CHUNK):
        ch = chunks[c]
        for p in range(N_PAIR):
            st = state_ref[p]
            o_ref[0, 0, ch["rows"], lanes(p)] = intra[c, p] + _dot_nt(ch["qd"][:, lanes(p)], st)
            state_ref[p] = (st * ch["p_end"][:, lanes(p)]
                            + _dot_tn(ch["i"][:, lanes(p)], ch["kd"][:, lanes(p)]) * head_blocks)


def _hgrn_scan(p_all, lb_logits, tables, layer):
    batch, n_tok, _ = p_all.shape
    n_blk = n_tok // TOK_BLOCK
    head0 = COL_HEADS // D_HGRN
    blk_of = _scan_block(n_blk)
    _, hg_masks, _, hg_sums = tables
    per_dir = lambda shape: pl.BlockSpec((1,) + shape, lambda b, d, j: (d,) + (0,) * len(shape))
    return pl.pallas_call(
        functools.partial(_hgrn_kernel, layer=layer),
        grid=(batch, 2, n_blk),
        in_specs=[
            pl.BlockSpec((1, TOK_BLOCK, D_HGRN), lambda b, d, j: (b, blk_of(d, j), head0 + 3)),
            pl.BlockSpec((1, TOK_BLOCK, D_HGRN), lambda b, d, j: (b, blk_of(d, j), head0 + 4)),
            pl.BlockSpec((1, TOK_BLOCK, D_HGRN), lambda b, d, j: (b, blk_of(d, j), head0 + 5 + d)),
            per_dir(lb_logits.shape[1:]), per_dir(hg_masks.shape[1:]), per_dir(hg_sums.shape[1:]),
        ],
        out_specs=pl.BlockSpec((1, 1, TOK_BLOCK, D_HGRN), lambda b, d, j: (d, b, blk_of(d, j), 0)),
        out_shape=jax.ShapeDtypeStruct((2, batch, n_tok, D_HGRN), _F32),
        scratch_shapes=[pltpu.VMEM((2, TOK_BLOCK, D_HGRN), _F32),
                        pltpu.VMEM((N_PAIR, PAIR, PAIR), _F32)],
        compiler_params=pltpu.CompilerParams(
            dimension_semantics=("parallel", "arbitrary", "arbitrary"),
            vmem_limit_bytes=VMEM_LIMIT_BYTES),
        name="hgrn_scan",
    )(p_all, p_all, p_all, lb_logits, hg_masks, hg_sums)


def _readout_kernel(h_ref, gate_ref, pool_ref, yf_ref, yb_ref, bf_ref, bb_ref, of_ref, ob_ref,
                    gt_ref, pm_ref, pw_ref, ps_ref, gng_ref, gnb_ref, ng_ref, hm_ref,
                    wout_ref, lng_ref, lnb_ref, out_ref, *, alpha):
    pv = pool_ref[0]
    group = lax.broadcasted_iota(jnp.int32, pv.shape, 1) // POOL_GROUP
    pooled = jnp.zeros_like(pv)
    for g in range(len(POOL_WINDOWS)):
        pooled = pooled + jnp.where(group == g, _dot(pm_ref[0, g], pv), 0.0)
    mixed = _dot(pooled, pw_ref[...]) * ps_ref[...]

    y = yf_ref[0, 0] + yb_ref[0, 0]
    yc = y - _dot(y, hm_ref[...])
    var = _dot(yc * yc, hm_ref[...])
    rw = yc * lax.rsqrt(var + GN_EPS) * gng_ref[...] + gnb_ref[...] + bf_ref[0, 0] + bb_ref[0, 0]

    o = of_ref[0, 0] + ob_ref[0, 0]
    ho = o * lax.rsqrt(_dot(o * o, hm_ref[...]) + RMS_EPS) * ng_ref[...]

    gate = gate_ref[0]
    mix = jnp.concatenate([mixed, rw, ho], axis=1) * (gate * jax.nn.sigmoid(gate))
    proj = jnp.dot(mix.astype(jnp.bfloat16), wout_ref[...], preferred_element_type=_F32)
    out_ref[0] = _layer_norm(alpha * h_ref[0] + gt_ref[0] * proj) * lng_ref[...] + lnb_ref[...]


def _readout(h, p_all, y, bonus, o, gate_mod, params, alpha, skip_ctx):
    batch, n_tok, _ = h.shape
    n_blk = n_tok // TOK_BLOCK
    off = 1 if skip_ctx else 0
    tok = lambda width, col: pl.BlockSpec((1, TOK_BLOCK, width), lambda b, j: (b, j + off, col))
    dirs = lambda d: pl.BlockSpec((1, 1, TOK_BLOCK, D_RWKV), lambda b, j: (d, b, j + off, 0))
    full = lambda shape: pl.BlockSpec(shape, lambda b, j: (0,) * len(shape))
    n_win = len(POOL_WINDOWS)
    return pl.pallas_call(
        functools.partial(_readout_kernel, alpha=alpha),
        grid=(batch, n_blk - off),
        in_specs=[
            tok(D_MODEL, 0), tok(D_MODEL, COL_GATE // D_MODEL), tok(D_POOL, COL_POOL // D_POOL),
            dirs(0), dirs(1), dirs(0), dirs(1), dirs(0), dirs(1),
            pl.BlockSpec((1, 1, D_MODEL), lambda b, j: (jnp.where(j + off == 0, batch, b), 0, 0)),
            pl.BlockSpec((1, n_win, TOK_BLOCK, TOK_BLOCK),
                         lambda b, j: (jnp.where(j + off == 0, 0, 1), 0, 0, 0)),
            full((D_POOL, D_POOL)), full((1, D_POOL)),
            full((1, D_RWKV)), full((1, D_RWKV)), full((1, D_HGRN)), full((D_RWKV, D_RWKV)),
            full((D_MODEL, D_MODEL)), full((1, D_MODEL)), full((1, D_MODEL)),
        ],
        out_specs=pl.BlockSpec((1, TOK_BLOCK, D_MODEL), lambda b, j: (b, j, 0)),
        out_shape=jax.ShapeDtypeStruct((batch, n_tok - off * TOK_BLOCK, D_MODEL), _F32),
        compiler_params=pltpu.CompilerParams(
            dimension_semantics=("parallel", "parallel"), vmem_limit_bytes=VMEM_LIMIT_BYTES),
        name="readout",
    )(h, p_all, p_all, y, y, bonus, bonus, o, o, gate_mod,
      jnp.asarray(_pool_tables()), params["pool_w"], params["pool_scale"],
      params["gn_g"], params["gn_b"], params["norm_g"], _head_mean_table(),
      params["w_out"], params["ln_g"], params["ln_b"])


def _block_diag(blocks):
    n, a, b = blocks.shape
    out = jnp.zeros((n * a, n * b), blocks.dtype)
    for g in range(n):
        out = out.at[g * a:(g + 1) * a, g * b:(g + 1) * b].set(blocks[g])
    return out


def _layer_params(l, w_in, rwkv_shift, pool_w, pool_scale, rwkv_w0, rwkv_w_up, rwkv_a0, rwkv_a_up,
                  rwkv_k_k, rwkv_k_a, rwkv_r_k, rwkv_gn_g, rwkv_gn_b, hgrn_norm_g, w_out, ln_g, ln_b):
    sh = rwkv_shift[l]
    lora_sh = jnp.stack([
        jnp.concatenate([sh[:, 1152 + d * LORA:1152 + (d + 1) * LORA],
                         sh[:, 1280 + d * LORA:1280 + (d + 1) * LORA]], axis=1) for d in range(2)])
    pad = jnp.zeros((2, LORA, D_RWKV), _F32)
    return {
        "w_in": w_in[l][:, _column_perm()].astype(jnp.bfloat16),
        "shift_r": sh[None, :, 0:384], "shift_k": sh[None, :, 384:768], "shift_v": sh[None, :, 768:1152],
        "shift_lora": lora_sh,
        "w0": rwkv_w0[l][:, None, :], "a0": rwkv_a0[l][:, None, :],
        "w_up": jnp.concatenate([rwkv_w_up[l], pad], axis=1),
        "a_up": jnp.concatenate([pad, rwkv_a_up[l]], axis=1),
        "k_k": rwkv_k_k[l][None], "k_a": rwkv_k_a[l][None],
        "r_k": rwkv_r_k[l].reshape(2, 1, D_RWKV),
        "pool_w": _block_diag(pool_w[l]), "pool_scale": pool_scale[l][None],
        "gn_g": rwkv_gn_g[l][None], "gn_b": rwkv_gn_b[l][None], "norm_g": hgrn_norm_g[l][None],
        "w_out": w_out[l].astype(jnp.bfloat16), "ln_g": ln_g[l][None], "ln_b": ln_b[l][None],
    }


def kernel(x, c, ctx, c_ctx, mod_w, mod_b, w_in, rwkv_shift, pool_w, pool_scale, rwkv_w0, rwkv_w_up,
           rwkv_a0, rwkv_a_up, rwkv_k_k, rwkv_k_a, rwkv_r_k, rwkv_gn_g, rwkv_gn_b, hgrn_lb_logits,
           hgrn_norm_g, w_out, ln_g, ln_b):
    depth = mod_w.shape[0]
    batch = x.shape[0]
    assert ctx.shape[1] == CTX_LEN == TOK_BLOCK and x.shape[1] % TOK_BLOCK == 0
    alpha = float((2 * depth) ** 0.25)
    tables = tuple(jnp.asarray(t) for t in _scan_tables())

    c_all = jnp.concatenate([c.astype(_F32), c_ctx.astype(_F32)[None]], axis=0)
    mod = _modulation(c_all, mod_w, mod_b)
    h = jnp.concatenate([ctx.astype(_F32), x.astype(_F32)], axis=1)

    for l in range(depth):
        prm = _layer_params(l, w_in, rwkv_shift, pool_w, pool_scale, rwkv_w0, rwkv_w_up, rwkv_a0,
                            rwkv_a_up, rwkv_k_k, rwkv_k_a, rwkv_r_k, rwkv_gn_g, rwkv_gn_b,
                            hgrn_norm_g, w_out, ln_g, ln_b)
        shift = mod[l, :, None, 0:D_MODEL]
        scale = mod[l, :, None, D_MODEL:2 * D_MODEL]
        gate_mod = mod[l, :, None, 2 * D_MODEL:]
        p_all = _input_projection(h, shift, scale, prm["w_in"])
        y, bonus = _rwkv_scan(p_all, prm, tables)
        o = _hgrn_scan(p_all, hgrn_lb_logits, tables, l)
        h = _readout(h, p_all, y, bonus, o, gate_mod, prm, alpha, skip_ctx=(l == depth - 1))
    return h.astype(x.dtype)
```

```python
import functools
import math

import numpy as np
import jax
import jax.numpy as jnp
from jax import lax
from jax.experimental import pallas as pl
from jax.experimental.pallas import tpu as pltpu

D_MODEL = 1024
CTX_LEN = 256
GRID_W = 64
HEAD_DIM = 64
D_POOL = 256
D_RWKV = 384
D_HGRN = 384
POOL_WINDOWS = (2, 4, 8, 16)
POOL_GROUP = 64
LORA = 64
D_IN = 4224
LN_EPS = 1e-5
GN_EPS = 64e-5
RMS_EPS = 1e-6

TOK_BLOCK = 256
CHUNK = 64
PAIR = 2 * HEAD_DIM
N_PAIR = D_RWKV // PAIR
BATCH_BLOCK = 4
CHUNK_LEVELS = (1, 2, 4, 8, 16, 32)
HALO = 8
VMEM_LIMIT_BYTES = 56 * 1024 * 1024

COL_GATE = 0
COL_POOL = 1024
COL_LORA = 1280
COL_HEADS = 1536

_F32 = jnp.float32
_HI = lax.Precision.HIGHEST


def _softplus(x):
    return jnp.maximum(x, 0.0) + jnp.log1p(jnp.exp(-jnp.abs(x)))


def _dot_exact(a, b):
    return jnp.dot(a, b, precision=_HI, preferred_element_type=_F32)


def _bf16(x):
    return x.astype(jnp.bfloat16)


def _dot(a, b):
    return jnp.dot(_bf16(a), _bf16(b), preferred_element_type=_F32)


def _dot_split(table, x):
    head = _bf16(x)
    rest = _bf16(x - head.astype(_F32))
    t = _bf16(table)
    return (jnp.dot(t, head, preferred_element_type=_F32) + jnp.dot(t, rest, preferred_element_type=_F32))


def _dot_nt(a, b):
    return lax.dot_general(_bf16(a), _bf16(b), (((1,), (1,)), ((), ())), preferred_element_type=_F32)


def _dot_tn(a, b):
    return lax.dot_general(_bf16(a), _bf16(b), (((0,), (0,)), ((), ())), preferred_element_type=_F32)


def _chunk_pos(direction):
    t = np.arange(CHUNK)
    return t if direction == 0 else CHUNK - 1 - t


def _level_tables(direction):
    pos = _chunk_pos(direction)
    masks, sums = [], []
    for m in CHUNK_LEVELS:
        blk = pos // m
        pair = pos // (2 * m)
        late = (blk % 2) == 1
        same = pair[:, None] == pair[None, :]
        masks.append(same & late[:, None] & (~late)[None, :])
        late_rows = same & late[:, None] & late[None, :] & (pos[None, :] <= pos[:, None])
        early_rows = same & (~late)[:, None] & (~late)[None, :] & (pos[None, :] > pos[:, None])
        sums.append(late_rows | early_rows)
    return masks, sums


def _tile2(m):
    return np.kron(np.eye(2, dtype=np.float32), m.astype(np.float32))


@functools.lru_cache(maxsize=None)
def _scan_tables():
    rw_masks, hg_masks, cums, hg_sums = [], [], [], []
    for d in range(2):
        pos = _chunk_pos(d)
        strict = pos[None, :] < pos[:, None]
        incl = pos[None, :] <= pos[:, None]
        after = pos[None, :] > pos[:, None]
        lv_masks, lv_sums = _level_tables(d)
        eye = np.eye(CHUNK, dtype=bool)
        rw_masks.append(np.stack([_tile2(strict), _tile2(incl)] + [_tile2(m) for m in lv_masks]
                                 + [np.eye(PAIR, dtype=np.float32)]))
        hg_masks.append(np.stack([_tile2(m) for m in lv_masks] + [_tile2(eye)]
                                 + [_tile2(np.ones((HEAD_DIM, HEAD_DIM)))]))
        cums.append(incl.astype(np.float32))
        hg_sums.append(np.concatenate([incl, after] + lv_sums, axis=0).astype(np.float32))
    return (np.stack(rw_masks), np.stack(hg_masks), np.stack(cums), np.stack(hg_sums))


@functools.lru_cache(maxsize=None)
def _pool_tables():
    out = np.zeros((2, len(POOL_WINDOWS), TOK_BLOCK, TOK_BLOCK), np.float32)
    for kind, seg in enumerate((TOK_BLOCK, GRID_W)):
        for g, win in enumerate(POOL_WINDOWS):
            left = win // 2
            right = win - 1 - left
            for t in range(TOK_BLOCK):
                base = (t // seg) * seg
                lo = max(t - left, base)
                hi = min(t + right, base + seg - 1) + 1
                out[kind, g, t, lo:hi] = 1.0 / (hi - lo)
                out[kind, g, t, t] -= 1.0
    return out


@functools.lru_cache(maxsize=None)
def _head_mean_table():
    h = np.arange(D_RWKV) // HEAD_DIM
    return (h[:, None] == h[None, :]).astype(np.float32) / HEAD_DIM


@functools.lru_cache(maxsize=None)
def _column_perm():
    pool = np.arange(0, 256)
    u0 = 256
    r = u0 + np.arange(0, 384)
    k = u0 + np.arange(384, 768)
    v = u0 + np.arange(768, 1152)
    lora = []
    for d in range(2):
        lora.append(u0 + 1152 + d * LORA + np.arange(LORA))
        lora.append(u0 + 1280 + d * LORA + np.arange(LORA))
    f0 = 256 + 1408
    q = f0 + np.arange(0, 384)
    i = f0 + np.arange(384, 768)
    z0 = f0 + np.arange(768, 1152)
    z1 = f0 + np.arange(1152, 1536)
    gate = np.arange(3200, 4224)
    perm = np.concatenate([gate, pool] + lora + [r, k, v, q, i, z0, z1])
    assert perm.shape[0] == D_IN and np.unique(perm).shape[0] == D_IN
    return perm


def _mod_kernel(c_ref, w_ref, b_ref, o_ref):
    c = c_ref[...]
    o_ref[0] = _dot_exact(c * jax.nn.sigmoid(c), w_ref[0]) + b_ref[0]


def _modulation(c_all, mod_w, mod_b):
    depth = mod_w.shape[0]
    rows = c_all.shape[0]
    n_tile = 1024
    return pl.pallas_call(
        _mod_kernel,
        grid=(depth, 3 * D_MODEL // n_tile),
        in_specs=[
            pl.BlockSpec((rows, D_MODEL), lambda l, n: (0, 0)),
            pl.BlockSpec((1, D_MODEL, n_tile), lambda l, n: (l, 0, n)),
            pl.BlockSpec((1, 1, n_tile), lambda l, n: (l, 0, n)),
        ],
        out_specs=pl.BlockSpec((1, rows, n_tile), lambda l, n: (l, 0, n)),
        out_shape=jax.ShapeDtypeStruct((depth, rows, 3 * D_MODEL), _F32),
        compiler_params=pltpu.CompilerParams(
            dimension_semantics=("arbitrary", "arbitrary"), vmem_limit_bytes=VMEM_LIMIT_BYTES),
        name="modulation",
    )(c_all, mod_w, mod_b.reshape(depth, 1, 3 * D_MODEL))


def _layer_norm(x):
    mu = jnp.mean(x, axis=-1, keepdims=True)
    xc = x - mu
    var = jnp.mean(xc * xc, axis=-1, keepdims=True)
    return xc * lax.rsqrt(var + LN_EPS)


def _proj_kernel(h_ref, sh_ref, sc_ref, w_ref, p_ref):
    xn = _layer_norm(h_ref[0]) * (1.0 + sc_ref[0]) + sh_ref[0]
    p_ref[0] = jnp.dot(xn.astype(jnp.bfloat16), w_ref[...], preferred_element_type=_F32)


def _mod_row(batch):
    return lambda b, j: (jnp.where(j == 0, batch, b), 0, 0)


def _input_projection(h, shift, scale, w_in_bf16):
    batch, n_tok, _ = h.shape
    return pl.pallas_call(
        _proj_kernel,
        grid=(batch, n_tok // TOK_BLOCK),
        in_specs=[
            pl.BlockSpec((1, TOK_BLOCK, D_MODEL), lambda b, j: (b, j, 0)),
            pl.BlockSpec((1, 1, D_MODEL), _mod_row(batch)),
            pl.BlockSpec((1, 1, D_MODEL), _mod_row(batch)),
            pl.BlockSpec((D_MODEL, D_IN), lambda b, j: (0, 0)),
        ],
        out_specs=pl.BlockSpec((1, TOK_BLOCK, D_IN), lambda b, j: (b, j, 0)),
        out_shape=jax.ShapeDtypeStruct((batch, n_tok, D_IN), _F32),
        compiler_params=pltpu.CompilerParams(
            dimension_semantics=("parallel", "parallel"), vmem_limit_bytes=VMEM_LIMIT_BYTES),
        name="input_projection",
    )(h, shift, scale, w_in_bf16)


def _scan_chunk(n_chunk):
    n_ctx = CTX_LEN // CHUNK
    return lambda d, j: jnp.where(d == 0, j, jnp.where(j < n_ctx, n_ctx - 1 - j, n_chunk + n_ctx - 1 - j))


def _batch_block(batch):
    return math.gcd(batch, BATCH_BLOCK)


def _stack_heads(x, lane_lo, lane_hi):
    return jnp.concatenate([x * lane_lo, x * lane_hi], axis=0)


def _unstack_heads(x2, lane_lo):
    return jnp.where(lane_lo > 0.5, x2[:CHUNK], x2[CHUNK:])


def _lane_masks():
    lane = lax.broadcasted_iota(jnp.int32, (CHUNK, PAIR), 1)
    lo = (lane < HEAD_DIM).astype(_F32)
    return lo, 1.0 - lo


def _twice(x):
    return jnp.concatenate([x, x], axis=0)


def _rwkv_kernel(r_ref, k_ref, v_ref, lo_ref,
                 rp_ref, kp_ref, vp_ref, lp_ref, rn_ref, kn_ref, vn_ref, ln_ref,
                 shr_ref, shk_ref, shv_ref, shl_ref,
                 w0_ref, wup_ref, a0_ref, aup_ref, kk_ref, ka_ref, rk_ref,
                 hm_ref, mask_ref, cum_ref,
                 y_ref, bonus_ref, state_ref, *, n_chunk, n_bb):
    d = pl.program_id(1)
    j = pl.program_id(2)
    chunk = _scan_chunk(n_chunk)(d, j)
    n_ctx = CTX_LEN // CHUNK
    prev_ok = jnp.where((chunk == 0) | (chunk == n_ctx), 0.0, 1.0).astype(_F32)
    next_ok = jnp.where((chunk == n_ctx - 1) | (chunk == n_chunk - 1), 0.0, 1.0).astype(_F32)

    @pl.when(j == 0)
    def _():
        state_ref[...] = jnp.zeros_like(state_ref)

    def shifted(main_ref, prev_ref, next_ref, w_ref):
        w = w_ref[0]
        outs = []
        for bb in range(n_bb):
            x = main_ref[bb]
            row = lax.broadcasted_iota(jnp.int32, x.shape, 0)
            prev_row = prev_ref[bb, HALO - 1:HALO, :] * prev_ok
            next_row = next_ref[bb, 0:1, :] * next_ok
            x_prev = jnp.where(row == 0, prev_row, pltpu.roll(x, 1, axis=0))
            x_next = jnp.where(row == CHUNK - 1, next_row, pltpu.roll(x, CHUNK - 1, axis=0))
            outs.append(w[0:1] * x_prev + w[1:2] * x + w[2:3] * x_next)
        return jnp.concatenate(outs, axis=0)

    r = shifted(r_ref, rp_ref, rn_ref, shr_ref)
    k = shifted(k_ref, kp_ref, kn_ref, shk_ref)
    v = shifted(v_ref, vp_ref, vn_ref, shv_ref)
    lora = shifted(lo_ref, lp_ref, ln_ref, shl_ref)

    w_pre = w0_ref[0] + _dot(jnp.tanh(lora), wup_ref[0])
    w_log = -_softplus(-w_pre) - 0.5
    lw = -jnp.exp(w_log)
    a = jax.nn.sigmoid(a0_ref[0] + _dot(lora, aup_ref[0]))
    kk = k * kk_ref[...]
    kk = kk * lax.rsqrt(_dot(kk * kk, hm_ref[...]) * HEAD_DIM + 1e-12)
    k_mod = k * (1.0 + (a - 1.0) * ka_ref[...])
    beta = a * kk
    bonus = _dot(r * k_mod * rk_ref[0], hm_ref[...]) * HEAD_DIM * v
    for bb in range(n_bb):
        bonus_ref[0, bb] = bonus[bb * CHUNK:(bb + 1) * CHUNK]

    lane_lo, lane_hi = _lane_masks()
    m_strict = mask_ref[0, 0]
    m_incl = mask_ref[0, 1]
    eye = mask_ref[0, 2 + len(CHUNK_LEVELS)]
    zeros = jnp.zeros((PAIR, PAIR), _F32)

    chunks = []
    for bb in range(n_bb):
        rows = slice(bb * CHUNK, (bb + 1) * CHUNK)
        lw_c = lw[rows]
        g_incl = _dot_split(cum_ref[0], lw_c)
        g_excl = g_incl - lw_c
        g_mid = g_incl[CHUNK // 2:CHUNK // 2 + 1, :]
        g_end = jnp.sum(lw_c, axis=0, keepdims=True)
        e_in = jnp.exp(g_incl - g_mid)
        e_out = jnp.exp(g_mid - g_incl)
        e_end = jnp.exp(g_end - g_incl)
        chunks.append(dict(
            p_end=jnp.exp(g_end),
            p_mid=jnp.exp(g_mid),
            kkd=kk[rows] * jnp.exp(g_excl - g_mid), rd=r[rows] * e_in,
            ki=k_mod[rows] * e_out, bi=beta[rows] * e_out,
            kd=k_mod[rows] * e_end, bd=beta[rows] * e_end, v=v[rows]))

    inst = [(bb, p) for bb in range(n_bb) for p in range(N_PAIR)]
    lanes = lambda p: slice(p * PAIR, (p + 1) * PAIR)
    nat = lambda bb, p, name: chunks[bb][name][:, lanes(p)]
    stk = lambda bb, p, name: _stack_heads(nat(bb, p, name), lane_lo, lane_hi)

    a_k, a_b, r_k, r_b = {}, {}, {}, {}
    for bb, p in inst:
        ki, bi = nat(bb, p, "ki"), nat(bb, p, "bi")
        gm = _dot_nt(jnp.concatenate([stk(bb, p, "kkd"), stk(bb, p, "rd")], axis=0),
                     jnp.concatenate([ki, ki, bi, bi], axis=0))
        a_k[bb, p] = gm[:PAIR, :PAIR] * m_strict
        a_b[bb, p] = gm[:PAIR, PAIR:] * m_strict
        r_k[bb, p] = gm[PAIR:, :PAIR] * m_incl
        r_b[bb, p] = gm[PAIR:, PAIR:] * m_incl

    t_inv = {ip: eye - a_b[ip] * mask_ref[0, 2] for ip in inst}
    w0 = {(bb, p): _dot(a_k[bb, p], stk(bb, p, "v")) for bb, p in inst}
    for lv in range(1, len(CHUNK_LEVELS)):
        x = {ip: _dot(t_inv[ip], a_b[ip] * mask_ref[0, 2 + lv]) for ip in inst}
        t_inv = {ip: t_inv[ip] - _dot(x[ip], t_inv[ip]) for ip in inst}

    p_mid = {(bb, p): chunks[bb]["p_mid"][:, lanes(p)] for bb, p in inst}
    z = {}
    for bb, p in inst:
        twk = _dot(t_inv[bb, p], jnp.concatenate([w0[bb, p], stk(bb, p, "kkd") * p_mid[bb, p]], axis=1))
        z[bb, p] = jnp.concatenate([jnp.concatenate([stk(bb, p, "v"), zeros], axis=1), twk], axis=0)
    yr = {ip: _dot(jnp.concatenate([r_k[ip], -r_b[ip]], axis=1), z[ip]) for ip in inst}
    mn = {(bb, p): _dot_tn(jnp.concatenate([stk(bb, p, "kd"), -stk(bb, p, "bd")], axis=0), z[bb, p])
          for bb, p in inst}
    st = {(bb, p): state_ref[bb * N_PAIR + p] for bb, p in inst}
    for bb, p in inst:
        y2 = yr[bb, p][:, :PAIR] + _dot(stk(bb, p, "rd") * p_mid[bb, p] + yr[bb, p][:, PAIR:], st[bb, p])
        y_ref[0, bb, :, lanes(p)] = _unstack_heads(y2, lane_lo)
    for bb, p in inst:
        m_c = mn[bb, p][:, PAIR:] + eye * chunks[bb]["p_end"][:, lanes(p)]
        state_ref[bb * N_PAIR + p] = _dot(m_c, st[bb, p]) + mn[bb, p][:, :PAIR]


def _rwkv_scan(p_all, params, tables):
    batch, n_tok, _ = p_all.shape
    n_bb = _batch_block(batch)
    n_chunk = n_tok // CHUNK
    n_halo = CHUNK // HALO
    head0 = COL_HEADS // D_RWKV
    lora0 = COL_LORA // PAIR
    chunk_of = _scan_chunk(n_chunk)

    def main(col_of, width=D_RWKV):
        return pl.BlockSpec((n_bb, CHUNK, width), lambda b, d, j: (b, chunk_of(d, j), col_of(d)))

    def prev(col_of, width=D_RWKV):
        return pl.BlockSpec(
            (n_bb, HALO, width),
            lambda b, d, j: (b, jnp.maximum(chunk_of(d, j) * n_halo - 1, 0), col_of(d)))

    def nxt(col_of, width=D_RWKV):
        return pl.BlockSpec(
            (n_bb, HALO, width),
            lambda b, d, j: (b, jnp.minimum((chunk_of(d, j) + 1) * n_halo, n_tok // HALO - 1), col_of(d)))

    col = lambda c: (lambda d: c)
    cols = [col(head0), col(head0 + 1), col(head0 + 2)]
    lora_col = lambda d: lora0 + d
    full = lambda shape: pl.BlockSpec(shape, lambda b, d, j: (0,) * len(shape))
    per_dir = lambda shape: pl.BlockSpec((1,) + shape, lambda b, d, j: (d,) + (0,) * len(shape))
    out_spec = pl.BlockSpec((1, n_bb, CHUNK, D_RWKV), lambda b, d, j: (d, b, chunk_of(d, j), 0))

    rw_masks, _, cums, _ = tables
    return pl.pallas_call(
        functools.partial(_rwkv_kernel, n_chunk=n_chunk, n_bb=n_bb),
        grid=(batch // n_bb, 2, n_chunk),
        in_specs=[main(c) for c in cols] + [main(lora_col, PAIR)]
        + [prev(c) for c in cols] + [prev(lora_col, PAIR)]
        + [nxt(c) for c in cols] + [nxt(lora_col, PAIR)]
        + [
            full((1, 3, D_RWKV)), full((1, 3, D_RWKV)), full((1, 3, D_RWKV)), per_dir((3, PAIR)),
            per_dir((1, D_RWKV)), per_dir((PAIR, D_RWKV)), per_dir((1, D_RWKV)), per_dir((PAIR, D_RWKV)),
            full((1, D_RWKV)), full((1, D_RWKV)), per_dir((1, D_RWKV)),
            full((D_RWKV, D_RWKV)), per_dir(rw_masks.shape[1:]), per_dir((CHUNK, CHUNK)),
        ],
        out_specs=[out_spec, out_spec],
        out_shape=[jax.ShapeDtypeStruct((2, batch, n_tok, D_RWKV), _F32)] * 2,
        scratch_shapes=[pltpu.VMEM((n_bb * N_PAIR, PAIR, PAIR), _F32)],
        compiler_params=pltpu.CompilerParams(
            dimension_semantics=("parallel", "arbitrary", "arbitrary"),
            vmem_limit_bytes=VMEM_LIMIT_BYTES),
        name="rwkv_scan",
    )(p_all, p_all, p_all, p_all, p_all, p_all, p_all, p_all, p_all, p_all, p_all, p_all,
      params["shift_r"], params["shift_k"], params["shift_v"], params["shift_lora"],
      params["w0"], params["w_up"], params["a0"], params["a_up"],
      params["k_k"], params["k_a"], params["r_k"],
      _head_mean_table(), rw_masks, cums)


def _hgrn_kernel(q_ref, i_ref, z_ref, lb_ref, mask_ref, sum_ref, o_ref, state_ref, *, layer, n_bb):
    j = pl.program_id(2)

    @pl.when(j == 0)
    def _():
        state_ref[...] = jnp.zeros_like(state_ref)

    logits = lb_ref[0]
    e = jnp.exp(logits - jnp.max(logits, axis=0, keepdims=True))
    lb_w = e / jnp.sum(e, axis=0, keepdims=True)
    csum = lb_w[0:1]
    for l in range(1, layer + 1):
        csum = csum + lb_w[l:l + 1]
    lb = jnp.maximum(csum - lb_w[0:1], 0.0)
    log_lb = jnp.log(lb)
    log_1m_lb = jnp.log1p(-lb)

    lane_lo, lane_hi = _lane_masks()
    n_lv = len(CHUNK_LEVELS)
    head_blocks = mask_ref[0, n_lv + 1]
    lv_mask = [mask_ref[0, n_lv]] + [mask_ref[0, lv] for lv in range(n_lv)]
    lanes = lambda p: slice(p * PAIR, (p + 1) * PAIR)
    scores, inter = {}, {}

    for bb in range(n_bb):
        z = z_ref[bb]
        log_rest = log_1m_lb - _softplus(-z)
        logf = jnp.maximum(log_lb, log_rest) + jnp.log1p(jnp.exp(-jnp.abs(log_lb - log_rest)))
        k_c = (1.0 - lb) * jax.nn.sigmoid(-z)
        q_c = q_ref[bb]
        i_c = i_ref[bb]
        sums = _dot_split(sum_ref[0], logf)
        g_incl = sums[:CHUNK]
        g_after = sums[CHUNK:2 * CHUNK]
        p_end = jnp.exp(g_incl[0:1] + g_after[0:1])
        qd = q_c * jnp.exp(g_incl)
        kd = k_c * jnp.exp(g_after)
        lv_q, lv_k = [q_c], [k_c]
        for lv in range(n_lv):
            x = jnp.exp(sums[(2 + lv) * CHUNK:(3 + lv) * CHUNK])
            lv_q.append(q_c * x)
            lv_k.append(k_c * x)
        for p in range(N_PAIR):
            sc = None
            for q_l, k_l, m_l in zip(lv_q, lv_k, lv_mask):
                s_l = _dot_nt(_stack_heads(q_l[:, lanes(p)], lane_lo, lane_hi), _twice(k_l[:, lanes(p)])) * m_l
                sc = s_l if sc is None else sc + s_l
            scores[bb, p] = sc
            st = state_ref[bb * N_PAIR + p]
            inter[bb, p] = _dot_nt(qd[:, lanes(p)], st)
            state_ref[bb * N_PAIR + p] = (st * p_end[:, lanes(p)]
                                          + _dot_tn(i_c[:, lanes(p)], kd[:, lanes(p)]) * head_blocks)

    for bb in range(n_bb):
        for p in range(N_PAIR):
            o2 = _dot(scores[bb, p], _twice(i_ref[bb][:, lanes(p)]))
            o_ref[0, bb, :, lanes(p)] = _unstack_heads(o2, lane_lo) + inter[bb, p]


def _hgrn_scan(p_all, lb_logits, tables, layer):
    batch, n_tok, _ = p_all.shape
    n_bb = _batch_block(batch)
    n_chunk = n_tok // CHUNK
    head0 = COL_HEADS // D_HGRN
    chunk_of = _scan_chunk(n_chunk)
    _, hg_masks, _, hg_sums = tables
    per_dir = lambda shape: pl.BlockSpec((1,) + shape, lambda b, d, j: (d,) + (0,) * len(shape))
    tok = lambda col_of: pl.BlockSpec((n_bb, CHUNK, D_HGRN), lambda b, d, j: (b, chunk_of(d, j), col_of(d)))
    return pl.pallas_call(
        functools.partial(_hgrn_kernel, layer=layer, n_bb=n_bb),
        grid=(batch // n_bb, 2, n_chunk),
        in_specs=[
            tok(lambda d: head0 + 3), tok(lambda d: head0 + 4), tok(lambda d: head0 + 5 + d),
            per_dir(lb_logits.shape[1:]), per_dir(hg_masks.shape[1:]), per_dir(hg_sums.shape[1:]),
        ],
        out_specs=pl.BlockSpec((1, n_bb, CHUNK, D_HGRN), lambda b, d, j: (d, b, chunk_of(d, j), 0)),
        out_shape=jax.ShapeDtypeStruct((2, batch, n_tok, D_HGRN), _F32),
        scratch_shapes=[pltpu.VMEM((n_bb * N_PAIR, PAIR, PAIR), _F32)],
        compiler_params=pltpu.CompilerParams(
            dimension_semantics=("parallel", "arbitrary", "arbitrary"),
            vmem_limit_bytes=VMEM_LIMIT_BYTES),
        name="hgrn_scan",
    )(p_all, p_all, p_all, lb_logits, hg_masks, hg_sums)


def _readout_kernel(h_ref, gate_ref, pool_ref, yf_ref, yb_ref, bf_ref, bb_ref, of_ref, ob_ref,
                    gt_ref, pm_ref, pw_ref, ps_ref, gng_ref, gnb_ref, ng_ref, hm_ref,
                    wout_ref, lng_ref, lnb_ref, out_ref, *, alpha):
    pv = pool_ref[0]
    group = lax.broadcasted_iota(jnp.int32, pv.shape, 1) // POOL_GROUP
    pooled = jnp.zeros_like(pv)
    for g in range(len(POOL_WINDOWS)):
        pooled = pooled + jnp.where(group == g, _dot(pm_ref[0, g], pv), 0.0)
    mixed = _dot(pooled, pw_ref[...]) * ps_ref[...]

    y = yf_ref[0, 0] + yb_ref[0, 0]
    yc = y - _dot(y, hm_ref[...])
    var = _dot(yc * yc, hm_ref[...])
    rw = yc * lax.rsqrt(var + GN_EPS) * gng_ref[...] + gnb_ref[...] + bf_ref[0, 0] + bb_ref[0, 0]

    o = of_ref[0, 0] + ob_ref[0, 0]
    ho = o * lax.rsqrt(_dot(o * o, hm_ref[...]) + RMS_EPS) * ng_ref[...]

    gate = gate_ref[0]
    mix = jnp.concatenate([mixed, rw, ho], axis=1) * (gate * jax.nn.sigmoid(gate))
    proj = jnp.dot(mix.astype(jnp.bfloat16), wout_ref[...], preferred_element_type=_F32)
    out_ref[0] = _layer_norm(alpha * h_ref[0] + gt_ref[0] * proj) * lng_ref[...] + lnb_ref[...]


def _readout(h, p_all, y, bonus, o, gate_mod, params, alpha, skip_ctx):
    batch, n_tok, _ = h.shape
    n_blk = n_tok // TOK_BLOCK
    off = 1 if skip_ctx else 0
    tok = lambda width, col: pl.BlockSpec((1, TOK_BLOCK, width), lambda b, j: (b, j + off, col))
    dirs = lambda d: pl.BlockSpec((1, 1, TOK_BLOCK, D_RWKV), lambda b, j: (d, b, j + off, 0))
    full = lambda shape: pl.BlockSpec(shape, lambda b, j: (0,) * len(shape))
    n_win = len(POOL_WINDOWS)
    return pl.pallas_call(
        functools.partial(_readout_kernel, alpha=alpha),
        grid=(batch, n_blk - off),
        in_specs=[
            tok(D_MODEL, 0), tok(D_MODEL, COL_GATE // D_MODEL), tok(D_POOL, COL_POOL // D_POOL),
            dirs(0), dirs(1), dirs(0), dirs(1), dirs(0), dirs(1),
            pl.BlockSpec((1, 1, D_MODEL), lambda b, j: (jnp.where(j + off == 0, batch, b), 0, 0)),
            pl.BlockSpec((1, n_win, TOK_BLOCK, TOK_BLOCK),
                         lambda b, j: (jnp.where(j + off == 0, 0, 1), 0, 0, 0)),
            full((D_POOL, D_POOL)), full((1, D_POOL)),
            full((1, D_RWKV)), full((1, D_RWKV)), full((1, D_HGRN)), full((D_RWKV, D_RWKV)),
            full((D_MODEL, D_MODEL)), full((1, D_MODEL)), full((1, D_MODEL)),
        ],
        out_specs=pl.BlockSpec((1, TOK_BLOCK, D_MODEL), lambda b, j: (b, j, 0)),
        out_shape=jax.ShapeDtypeStruct((batch, n_tok - off * TOK_BLOCK, D_MODEL), _F32),
        compiler_params=pltpu.CompilerParams(
            dimension_semantics=("parallel", "parallel"), vmem_limit_bytes=VMEM_LIMIT_BYTES),
        name="readout",
    )(h, p_all, p_all, y, y, bonus, bonus, o, o, gate_mod,
      jnp.asarray(_pool_tables()), params["pool_w"], params["pool_scale"],
      params["gn_g"], params["gn_b"], params["norm_g"], _head_mean_table(),
      params["w_out"], params["ln_g"], params["ln_b"])


def _block_diag(blocks):
    n, a, b = blocks.shape
    out = jnp.zeros((n * a, n * b), blocks.dtype)
    for g in range(n):
        out = out.at[g * a:(g + 1) * a, g * b:(g + 1) * b].set(blocks[g])
    return out


def _layer_params(l, w_in, rwkv_shift, pool_w, pool_scale, rwkv_w0, rwkv_w_up, rwkv_a0, rwkv_a_up,
                  rwkv_k_k, rwkv_k_a, rwkv_r_k, rwkv_gn_g, rwkv_gn_b, hgrn_norm_g, w_out, ln_g, ln_b):
    sh = rwkv_shift[l]
    lora_sh = jnp.stack([
        jnp.concatenate([sh[:, 1152 + d * LORA:1152 + (d + 1) * LORA],
                         sh[:, 1280 + d * LORA:1280 + (d + 1) * LORA]], axis=1) for d in range(2)])
    pad = jnp.zeros((2, LORA, D_RWKV), _F32)
    return {
        "w_in": w_in[l][:, _column_perm()].astype(jnp.bfloat16),
        "shift_r": sh[None, :, 0:384], "shift_k": sh[None, :, 384:768], "shift_v": sh[None, :, 768:1152],
        "shift_lora": lora_sh,
        "w0": rwkv_w0[l][:, None, :], "a0": rwkv_a0[l][:, None, :],
        "w_up": jnp.concatenate([rwkv_w_up[l], pad], axis=1),
        "a_up": jnp.concatenate([pad, rwkv_a_up[l]], axis=1),
        "k_k": rwkv_k_k[l][None], "k_a": rwkv_k_a[l][None],
        "r_k": rwkv_r_k[l].reshape(2, 1, D_RWKV),
        "pool_w": _block_diag(pool_w[l]), "pool_scale": pool_scale[l][None],
        "gn_g": rwkv_gn_g[l][None], "gn_b": rwkv_gn_b[l][None], "norm_g": hgrn_norm_g[l][None],
        "w_out": w_out[l].astype(jnp.bfloat16), "ln_g": ln_g[l][None], "ln_b": ln_b[l][None],
    }


def kernel(x, c, ctx, c_ctx, mod_w, mod_b, w_in, rwkv_shift, pool_w, pool_scale, rwkv_w0, rwkv_w_up,
           rwkv_a0, rwkv_a_up, rwkv_k_k, rwkv_k_a, rwkv_r_k, rwkv_gn_g, rwkv_gn_b, hgrn_lb_logits,
           hgrn_norm_g, w_out, ln_g, ln_b):
    depth = mod_w.shape[0]
    batch = x.shape[0]
    assert ctx.shape[1] == CTX_LEN == TOK_BLOCK and x.shape[1] % TOK_BLOCK == 0
    alpha = float((2 * depth) ** 0.25)
    tables = tuple(jnp.asarray(t) for t in _scan_tables())

    c_all = jnp.concatenate([c.astype(_F32), c_ctx.astype(_F32)[None]], axis=0)
    mod = _modulation(c_all, mod_w, mod_b)
    h = jnp.concatenate([ctx.astype(_F32), x.astype(_F32)], axis=1)

    for l in range(depth):
        prm = _layer_params(l, w_in, rwkv_shift, pool_w, pool_scale, rwkv_w0, rwkv_w_up, rwkv_a0,
                            rwkv_a_up, rwkv_k_k, rwkv_k_a, rwkv_r_k, rwkv_gn_g, rwkv_gn_b,
                            hgrn_norm_g, w_out, ln_g, ln_b)
        shift = mod[l, :, None, 0:D_MODEL]
        scale = mod[l, :, None, D_MODEL:2 * D_MODEL]
        gate_mod = mod[l, :, None, 2 * D_MODEL:]
        p_all = _input_projection(h, shift, scale, prm["w_in"])
        y, bonus = _rwkv_scan(p_all, prm, tables)
        o = _hgrn_scan(p_all, hgrn_lb_logits, tables, l)
        h = _readout(h, p_all, y, bonus, o, gate_mod, prm, alpha, skip_ctx=(l == depth - 1))
    return h.astype(x.dtype)
```

```python
import functools
import math

import numpy as np
import jax
import jax.numpy as jnp
from jax import lax
from jax.experimental import pallas as pl
from jax.experimental.pallas import tpu as pltpu

D_MODEL = 1024
CTX_LEN = 256
GRID_W = 64
HEAD_DIM = 64
D_POOL = 256
D_RWKV = 384
D_HGRN = 384
POOL_WINDOWS = (2, 4, 8, 16)
POOL_GROUP = 64
LORA = 64
D_IN = 4224
LN_EPS = 1e-5
GN_EPS = 64e-5
RMS_EPS = 1e-6

TOK_BLOCK = 256
CHUNK = 64
PAIR = 2 * HEAD_DIM
N_PAIR = D_RWKV // PAIR
BATCH_BLOCK = 8
PREP_GROUP = 4
WAVE_SKEW = 1
CHUNK_LEVELS = (1, 2, 4, 8, 16, 32)
HALO = 8
VMEM_LIMIT_BYTES = 56 * 1024 * 1024

COL_GATE = 0
COL_POOL = 1024
COL_LORA = 1280
COL_HEADS = 1536

_F32 = jnp.float32
_HI = lax.Precision.HIGHEST


def _softplus(x):
    return jnp.maximum(x, 0.0) + jnp.log1p(jnp.exp(-jnp.abs(x)))


def _dot_exact(a, b):
    return jnp.dot(a, b, precision=_HI, preferred_element_type=_F32)


def _bf16(x):
    return x.astype(jnp.bfloat16)


def _dot(a, b):
    return jnp.dot(_bf16(a), _bf16(b), preferred_element_type=_F32)


def _dot_split(table2, x):
    head = _bf16(x)
    rest = _bf16(x - head.astype(_F32))
    return jnp.dot(_bf16(table2), jnp.concatenate([head, rest], axis=0), preferred_element_type=_F32)


def _dot_nt(a, b):
    return lax.dot_general(_bf16(a), _bf16(b), (((1,), (1,)), ((), ())), preferred_element_type=_F32)


def _dot_tn(a, b):
    return lax.dot_general(_bf16(a), _bf16(b), (((0,), (0,)), ((), ())), preferred_element_type=_F32)


def _chunk_pos(direction):
    t = np.arange(CHUNK)
    return t if direction == 0 else CHUNK - 1 - t


def _level_tables(direction):
    pos = _chunk_pos(direction)
    masks, sums = [], []
    for m in CHUNK_LEVELS:
        blk = pos // m
        pair = pos // (2 * m)
        late = (blk % 2) == 1
        same = pair[:, None] == pair[None, :]
        masks.append(same & late[:, None] & (~late)[None, :])
        late_rows = same & late[:, None] & late[None, :] & (pos[None, :] <= pos[:, None])
        early_rows = same & (~late)[:, None] & (~late)[None, :] & (pos[None, :] > pos[:, None])
        sums.append(late_rows | early_rows)
    return masks, sums


def _tile2(m):
    return np.kron(np.eye(2, dtype=np.float32), m.astype(np.float32))


@functools.lru_cache(maxsize=None)
def _scan_tables():
    rw_masks, hg_masks, cums, hg_sums = [], [], [], []
    for d in range(2):
        pos = _chunk_pos(d)
        strict = pos[None, :] < pos[:, None]
        incl = pos[None, :] <= pos[:, None]
        lv_masks, lv_sums = _level_tables(d)
        eye = np.eye(CHUNK, dtype=bool)
        rw_masks.append(np.stack([_tile2(strict), _tile2(incl)] + [_tile2(m) for m in lv_masks]
                                 + [-_tile2(strict)]))
        hg_masks.append(np.stack([_tile2(m) for m in lv_masks] + [_tile2(eye)]
                                 + [_tile2(np.ones((HEAD_DIM, HEAD_DIM)))]))
        cums.append(np.tile(incl.astype(np.float32), (1, 2)))
        hg_sums.append(np.tile(np.concatenate([incl] + lv_sums, axis=0).astype(np.float32), (1, 2)))
    return (np.stack(rw_masks), np.stack(hg_masks), np.stack(cums), np.stack(hg_sums))


@functools.lru_cache(maxsize=None)
def _pool_tables():
    out = np.zeros((2, len(POOL_WINDOWS), TOK_BLOCK, TOK_BLOCK), np.float32)
    for kind, seg in enumerate((TOK_BLOCK, GRID_W)):
        for g, win in enumerate(POOL_WINDOWS):
            left = win // 2
            right = win - 1 - left
            for t in range(TOK_BLOCK):
                base = (t // seg) * seg
                lo = max(t - left, base)
                hi = min(t + right, base + seg - 1) + 1
                out[kind, g, t, lo:hi] = 1.0 / (hi - lo)
                out[kind, g, t, t] -= 1.0
    return out


@functools.lru_cache(maxsize=None)
def _head_mean_table():
    h = np.arange(D_RWKV) // HEAD_DIM
    return (h[:, None] == h[None, :]).astype(np.float32) / HEAD_DIM


@functools.lru_cache(maxsize=None)
def _column_perm():
    pool = np.arange(0, 256)
    u0 = 256
    r = u0 + np.arange(0, 384)
    k = u0 + np.arange(384, 768)
    v = u0 + np.arange(768, 1152)
    lora = []
    for d in range(2):
        lora.append(u0 + 1152 + d * LORA + np.arange(LORA))
        lora.append(u0 + 1280 + d * LORA + np.arange(LORA))
    f0 = 256 + 1408
    q = f0 + np.arange(0, 384)
    i = f0 + np.arange(384, 768)
    z0 = f0 + np.arange(768, 1152)
    z1 = f0 + np.arange(1152, 1536)
    gate = np.arange(3200, 4224)
    perm = np.concatenate([gate, pool] + lora + [r, k, v, q, i, z0, z1])
    assert perm.shape[0] == D_IN and np.unique(perm).shape[0] == D_IN
    return perm


def _mod_kernel(c_ref, w_ref, b_ref, o_ref):
    c = c_ref[...]
    o_ref[0] = _dot_exact(c * jax.nn.sigmoid(c), w_ref[0]) + b_ref[0]


def _modulation(c_all, mod_w, mod_b):
    depth = mod_w.shape[0]
    rows = c_all.shape[0]
    n_tile = 1024
    return pl.pallas_call(
        _mod_kernel,
        grid=(depth, 3 * D_MODEL // n_tile),
        in_specs=[
            pl.BlockSpec((rows, D_MODEL), lambda l, n: (0, 0)),
            pl.BlockSpec((1, D_MODEL, n_tile), lambda l, n: (l, 0, n)),
            pl.BlockSpec((1, 1, n_tile), lambda l, n: (l, 0, n)),
        ],
        out_specs=pl.BlockSpec((1, rows, n_tile), lambda l, n: (l, 0, n)),
        out_shape=jax.ShapeDtypeStruct((depth, rows, 3 * D_MODEL), _F32),
        compiler_params=pltpu.CompilerParams(
            dimension_semantics=("arbitrary", "arbitrary"), vmem_limit_bytes=VMEM_LIMIT_BYTES),
        name="modulation",
    )(c_all, mod_w, mod_b.reshape(depth, 1, 3 * D_MODEL))


def _layer_norm(x):
    mu = jnp.mean(x, axis=-1, keepdims=True)
    xc = x - mu
    var = jnp.mean(xc * xc, axis=-1, keepdims=True)
    return xc * lax.rsqrt(var + LN_EPS)


def _proj_kernel(h_ref, sh_ref, sc_ref, w_ref, p_ref):
    xn = _layer_norm(h_ref[0]) * (1.0 + sc_ref[0]) + sh_ref[0]
    p_ref[0] = jnp.dot(xn.astype(jnp.bfloat16), w_ref[...], preferred_element_type=_F32)


def _mod_row(batch):
    return lambda b, j: (jnp.where(j == 0, batch, b), 0, 0)


def _input_projection(h, shift, scale, w_in_bf16):
    batch, n_tok, _ = h.shape
    return pl.pallas_call(
        _proj_kernel,
        grid=(batch, n_tok // TOK_BLOCK),
        in_specs=[
            pl.BlockSpec((1, TOK_BLOCK, D_MODEL), lambda b, j: (b, j, 0)),
            pl.BlockSpec((1, 1, D_MODEL), _mod_row(batch)),
            pl.BlockSpec((1, 1, D_MODEL), _mod_row(batch)),
            pl.BlockSpec((D_MODEL, D_IN), lambda b, j: (0, 0)),
        ],
        out_specs=pl.BlockSpec((1, TOK_BLOCK, D_IN), lambda b, j: (b, j, 0)),
        out_shape=jax.ShapeDtypeStruct((batch, n_tok, D_IN), _F32),
        compiler_params=pltpu.CompilerParams(
            dimension_semantics=("parallel", "parallel"), vmem_limit_bytes=VMEM_LIMIT_BYTES),
        name="input_projection",
    )(h, shift, scale, w_in_bf16)


def _scan_chunk(n_chunk):
    n_ctx = CTX_LEN // CHUNK
    return lambda d, j: jnp.where(d == 0, j, jnp.where(j < n_ctx, n_ctx - 1 - j, n_chunk + n_ctx - 1 - j))


def _batch_block(batch):
    return math.gcd(batch, BATCH_BLOCK)


def _stack_heads(x, lane_lo, lane_hi):
    return jnp.concatenate([x * lane_lo, x * lane_hi], axis=0)


def _unstack_heads(x2, lane_lo):
    return jnp.where(lane_lo > 0.5, x2[:CHUNK], x2[CHUNK:])


def _lane_masks():
    lane = lax.broadcasted_iota(jnp.int32, (CHUNK, PAIR), 1)
    lo = (lane < HEAD_DIM).astype(_F32)
    return lo, 1.0 - lo


def _twice(x):
    return jnp.concatenate([x, x], axis=0)


def _run_wavefront(programs):
    live = [True] * len(programs)
    rnd = 0
    while any(live):
        for g in reversed(range(len(programs))):
            if live[g] and rnd >= WAVE_SKEW * g:
                try:
                    next(programs[g])
                except StopIteration:
                    live[g] = False
        rnd += 1


def _rwkv_kernel(r_ref, k_ref, v_ref, lo_ref,
                 rp_ref, kp_ref, vp_ref, lp_ref, rn_ref, kn_ref, vn_ref, ln_ref,
                 shr_ref, shk_ref, shv_ref, shl_ref,
                 w0_ref, wup_ref, a0_ref, aup_ref, kk_ref, ka_ref, rk_ref,
                 hm_ref, mask_ref, cum_ref,
                 y_ref, bonus_ref, state_ref, *, n_chunk, n_bb):
    d = pl.program_id(1)
    j = pl.program_id(2)
    chunk = _scan_chunk(n_chunk)(d, j)
    n_ctx = CTX_LEN // CHUNK
    prev_ok = jnp.where((chunk == 0) | (chunk == n_ctx), 0.0, 1.0).astype(_F32)
    next_ok = jnp.where((chunk == n_ctx - 1) | (chunk == n_chunk - 1), 0.0, 1.0).astype(_F32)

    @pl.when(j == 0)
    def _():
        state_ref[...] = jnp.zeros_like(state_ref)

    def shifted(main_ref, prev_ref, next_ref, w_ref, bbs):
        w = w_ref[0]
        outs = []
        for bb in bbs:
            x = main_ref[bb]
            row = lax.broadcasted_iota(jnp.int32, x.shape, 0)
            prev_row = prev_ref[bb, HALO - 1:HALO, :] * prev_ok
            next_row = next_ref[bb, 0:1, :] * next_ok
            x_prev = jnp.where(row == 0, prev_row, pltpu.roll(x, 1, axis=0))
            x_next = jnp.where(row == CHUNK - 1, next_row, pltpu.roll(x, CHUNK - 1, axis=0))
            outs.append(w[0:1] * x_prev + w[1:2] * x + w[2:3] * x_next)
        return jnp.concatenate(outs, axis=0)

    group_size = math.gcd(n_bb, PREP_GROUP)
    groups = {}

    def group_prep(g):
        if g in groups:
            return groups[g]
        bbs = range(g * group_size, (g + 1) * group_size)
        r = shifted(r_ref, rp_ref, rn_ref, shr_ref, bbs)
        k = shifted(k_ref, kp_ref, kn_ref, shk_ref, bbs)
        v = shifted(v_ref, vp_ref, vn_ref, shv_ref, bbs)
        lora = shifted(lo_ref, lp_ref, ln_ref, shl_ref, bbs)
        w_pre = w0_ref[0] + _dot(jnp.tanh(lora), wup_ref[0])
        w_log = -_softplus(-w_pre) - 0.5
        lw = -jnp.exp(w_log)
        a = jax.nn.sigmoid(a0_ref[0] + _dot(lora, aup_ref[0]))
        kk = k * kk_ref[...]
        kk = kk * lax.rsqrt(_dot(kk * kk, hm_ref[...]) * HEAD_DIM + 1e-12)
        k_mod = k * (1.0 + (a - 1.0) * ka_ref[...])
        bonus = _dot(r * k_mod * rk_ref[0], hm_ref[...]) * HEAD_DIM * v
        for i, bb in enumerate(bbs):
            bonus_ref[0, bb] = bonus[i * CHUNK:(i + 1) * CHUNK]
        groups[g] = dict(r=r, v=v, lw=lw, kk=kk, k_mod=k_mod, beta=a * kk)
        return groups[g]

    lane = lax.broadcasted_iota(jnp.int32, (CHUNK, PAIR), 1)
    lane_lo = lane < HEAD_DIM
    lo16 = lane_lo.astype(jnp.bfloat16)
    hi16 = 1.0 - lo16
    lane_lo32, lane_hi32 = _lane_masks()
    square = lax.broadcasted_iota(jnp.int32, (PAIR, PAIR), 0) == lax.broadcasted_iota(jnp.int32, (PAIR, PAIR), 1)
    eye = square.astype(_F32)
    zeros16 = jnp.zeros((PAIR, PAIR), jnp.bfloat16)
    m_strict = mask_ref[0, 0]
    m_incl2 = jnp.concatenate([mask_ref[0, 1], mask_ref[0, 1]], axis=1)
    m_neg_strict = mask_ref[0, 2 + len(CHUNK_LEVELS)]
    lv_masks = [mask_ref[0, 2 + lv] for lv in range(len(CHUNK_LEVELS))]
    cum = cum_ref[0]
    lanes = lambda p: slice(p * PAIR, (p + 1) * PAIR)
    pairs = range(N_PAIR)
    results = []

    def program(bb):
        tok = group_prep(bb // group_size)
        rows = slice((bb % group_size) * CHUNK, (bb % group_size + 1) * CHUNK)
        lw_c = tok["lw"][rows]
        g_incl = _dot_split(cum, lw_c)
        yield
        g_excl = g_incl - lw_c
        g_mid = g_incl[CHUNK // 2:CHUNK // 2 + 1, :]
        g_end = jnp.sum(lw_c, axis=0, keepdims=True)
        e_in = jnp.exp(g_incl - g_mid)
        e_out = jnp.exp(g_mid - g_incl)
        e_end = jnp.exp(g_end - g_incl)
        p_end = jnp.exp(g_end)
        p_mid = jnp.exp(g_mid)
        kkd = tok["kk"][rows] * jnp.exp(g_excl - g_mid)
        rd = tok["r"][rows] * e_in
        k_c, b_c = tok["k_mod"][rows], tok["beta"][rows]
        ch = {name: _bf16(val) for name, val in dict(
            kkd=kkd, kkd_n=kkd * (-p_mid), rd=rd, ki=k_c * e_out, bi=b_c * e_out,
            kd=k_c * e_end, bd=b_c * e_end, v=tok["v"][rows]).items()}
        nat = lambda p, name: ch[name][:, lanes(p)]
        stk = lambda p, name: jnp.concatenate([nat(p, name) * lo16, nat(p, name) * hi16], axis=0)
        rd_abs = rd * p_mid

        gm = [_dot_nt(jnp.concatenate([stk(p, "kkd"), stk(p, "rd")], axis=0),
                      jnp.concatenate([nat(p, "ki"), nat(p, "ki"), nat(p, "bi"), nat(p, "bi")], axis=0))
              for p in pairs]
        yield
        g16 = [_bf16(g) for g in gm]
        a_kn = [g[:PAIR, :PAIR] * m_neg_strict for g in g16]
        a_b = [g[:PAIR, PAIR:] * m_strict for g in g16]
        r_kb = [g[PAIR:, :] * m_incl2 for g in g16]
        v2 = [stk(p, "v") for p in pairs]
        t_inv = [eye - (a_b[p] * lv_masks[0]).astype(_F32) for p in pairs]
        w0n = [_dot(a_kn[p], v2[p]) for p in pairs]
        for lv in range(1, len(CHUNK_LEVELS)):
            t16 = [_bf16(t) for t in t_inv]
            x = [_dot(t16[p], a_b[p] * lv_masks[lv]) for p in pairs]
            yield
            t_new = [_dot(x[p], t16[p]) for p in pairs]
            yield
            t_inv = [t_inv[p] - t_new[p] for p in pairs]
        twk = [_dot(t_inv[p], jnp.concatenate([_bf16(w0n[p]), stk(p, "kkd_n")], axis=1)) for p in pairs]
        yield
        z = [jnp.concatenate([jnp.concatenate([v2[p], zeros16], axis=1), _bf16(twk[p])], axis=0) for p in pairs]
        yr = [_dot(r_kb[p], z[p]) for p in pairs]
        mn = [_dot_tn(jnp.concatenate([stk(p, "kd"), stk(p, "bd")], axis=0), z[p]) for p in pairs]
        yield
        st = [state_ref[bb * N_PAIR + p] for p in pairs]
        rq = [_stack_heads(rd_abs[:, lanes(p)], lane_lo32, lane_hi32) + yr[p][:, PAIR:] for p in pairs]
        y2 = [yr[p][:, :PAIR] + _dot(rq[p], st[p]) for p in pairs]
        st_new = [_dot(mn[p][:, PAIR:] + eye * p_end[:, lanes(p)], st[p]) + mn[p][:, :PAIR] for p in pairs]
        results.append((bb, y2, st_new))

    _run_wavefront([program(bb) for bb in range(n_bb)])
    for bb, y2, st_new in results:
        for p in pairs:
            y_ref[0, bb, :, lanes(p)] = _unstack_heads(y2[p], lane_lo32)
            state_ref[bb * N_PAIR + p] = st_new[p]


def _rwkv_scan(p_all, params, tables):
    batch, n_tok, _ = p_all.shape
    n_bb = _batch_block(batch)
    n_chunk = n_tok // CHUNK
    n_halo = CHUNK // HALO
    head0 = COL_HEADS // D_RWKV
    lora0 = COL_LORA // PAIR
    chunk_of = _scan_chunk(n_chunk)

    def main(col_of, width=D_RWKV):
        return pl.BlockSpec((n_bb, CHUNK, width), lambda b, d, j: (b, chunk_of(d, j), col_of(d)))

    def prev(col_of, width=D_RWKV):
        return pl.BlockSpec(
            (n_bb, HALO, width),
            lambda b, d, j: (b, jnp.maximum(chunk_of(d, j) * n_halo - 1, 0), col_of(d)))

    def nxt(col_of, width=D_RWKV):
        return pl.BlockSpec(
            (n_bb, HALO, width),
            lambda b, d, j: (b, jnp.minimum((chunk_of(d, j) + 1) * n_halo, n_tok // HALO - 1), col_of(d)))

    col = lambda c: (lambda d: c)
    cols = [col(head0), col(head0 + 1), col(head0 + 2)]
    lora_col = lambda d: lora0 + d
    full = lambda shape: pl.BlockSpec(shape, lambda b, d, j: (0,) * len(shape))
    per_dir = lambda shape: pl.BlockSpec((1,) + shape, lambda b, d, j: (d,) + (0,) * len(shape))
    out_spec = pl.BlockSpec((1, n_bb, CHUNK, D_RWKV), lambda b, d, j: (d, b, chunk_of(d, j), 0))

    rw_masks, _, cums, _ = tables
    rw_masks = rw_masks.astype(jnp.bfloat16)
    return pl.pallas_call(
        functools.partial(_rwkv_kernel, n_chunk=n_chunk, n_bb=n_bb),
        grid=(batch // n_bb, 2, n_chunk),
        in_specs=[main(c) for c in cols] + [main(lora_col, PAIR)]
        + [prev(c) for c in cols] + [prev(lora_col, PAIR)]
        + [nxt(c) for c in cols] + [nxt(lora_col, PAIR)]
        + [
            full((1, 3, D_RWKV)), full((1, 3, D_RWKV)), full((1, 3, D_RWKV)), per_dir((3, PAIR)),
            per_dir((1, D_RWKV)), per_dir((PAIR, D_RWKV)), per_dir((1, D_RWKV)), per_dir((PAIR, D_RWKV)),
            full((1, D_RWKV)), full((1, D_RWKV)), per_dir((1, D_RWKV)),
            full((D_RWKV, D_RWKV)), per_dir(rw_masks.shape[1:]), per_dir(cums.shape[1:]),
        ],
        out_specs=[out_spec, out_spec],
        out_shape=[jax.ShapeDtypeStruct((2, batch, n_tok, D_RWKV), _F32)] * 2,
        scratch_shapes=[pltpu.VMEM((n_bb * N_PAIR, PAIR, PAIR), _F32)],
        compiler_params=pltpu.CompilerParams(
            dimension_semantics=("parallel", "arbitrary", "arbitrary"),
            vmem_limit_bytes=VMEM_LIMIT_BYTES),
        name="rwkv_scan",
    )(p_all, p_all, p_all, p_all, p_all, p_all, p_all, p_all, p_all, p_all, p_all, p_all,
      params["shift_r"], params["shift_k"], params["shift_v"], params["shift_lora"],
      params["w0"], params["w_up"], params["a0"], params["a_up"],
      params["k_k"], params["k_a"], params["r_k"],
      _head_mean_table(), rw_masks, cums)


def _hgrn_kernel(q_ref, i_ref, z_ref, lb_ref, mask_ref, sum_ref, o_ref, state_ref, *, layer, n_bb):
    j = pl.program_id(2)

    @pl.when(j == 0)
    def _():
        state_ref[...] = jnp.zeros_like(state_ref)

    logits = lb_ref[0]
    e = jnp.exp(logits - jnp.max(logits, axis=0, keepdims=True))
    lb_w = e / jnp.sum(e, axis=0, keepdims=True)
    csum = lb_w[0:1]
    for l in range(1, layer + 1):
        csum = csum + lb_w[l:l + 1]
    lb = jnp.maximum(csum - lb_w[0:1], 0.0)
    log_lb = jnp.log(lb)
    log_1m_lb = jnp.log1p(-lb)

    lane_lo, lane_hi = _lane_masks()
    n_lv = len(CHUNK_LEVELS)
    head_blocks = mask_ref[0, n_lv + 1]
    lv_mask = [mask_ref[0, n_lv]] + [mask_ref[0, lv] for lv in range(n_lv)]
    lanes = lambda p: slice(p * PAIR, (p + 1) * PAIR)
    sum_table = sum_ref[0]
    pairs = range(N_PAIR)
    results = []

    def program(bb):
        z = z_ref[bb]
        log_rest = log_1m_lb - _softplus(-z)
        logf = jnp.maximum(log_lb, log_rest) + jnp.log1p(jnp.exp(-jnp.abs(log_lb - log_rest)))
        k_c = (1.0 - lb) * jax.nn.sigmoid(-z)
        q_c = q_ref[bb]
        i_c = i_ref[bb]
        sums = _dot_split(sum_table, logf)
        yield
        g_incl = sums[:CHUNK]
        g_end = jnp.min(g_incl, axis=0, keepdims=True)
        p_end = jnp.exp(g_end)
        qd = q_c * jnp.exp(g_incl)
        kd = k_c * jnp.exp(g_end - g_incl)
        st = [state_ref[bb * N_PAIR + p] for p in pairs]
        inter = [_dot_nt(qd[:, lanes(p)], st[p]) for p in pairs]
        outer = [_dot_tn(i_c[:, lanes(p)], kd[:, lanes(p)]) for p in pairs]
        lv_q, lv_k = [q_c], [k_c]
        for lv in range(n_lv):
            x = jnp.exp(sums[(1 + lv) * CHUNK:(2 + lv) * CHUNK])
            lv_q.append(q_c * x)
            lv_k.append(k_c * x)
        scores = []
        for p in pairs:
            parts = [_dot_nt(_stack_heads(q_l[:, lanes(p)], lane_lo, lane_hi), _twice(k_l[:, lanes(p)]))
                     for q_l, k_l in zip(lv_q, lv_k)]
            yield
            sc = parts[0] * lv_mask[0]
            for part, m_l in zip(parts[1:], lv_mask[1:]):
                sc = sc + part * m_l
            scores.append(sc)
        o2 = [_dot(scores[p], _twice(i_c[:, lanes(p)])) for p in pairs]
        st_new = [st[p] * p_end[:, lanes(p)] + outer[p] * head_blocks for p in pairs]
        yield
        results.append((bb, [_unstack_heads(o2[p], lane_lo) + inter[p] for p in pairs], st_new))

    _run_wavefront([program(bb) for bb in range(n_bb)])
    for bb, o_new, st_new in results:
        for p in pairs:
            o_ref[0, bb, :, lanes(p)] = o_new[p]
            state_ref[bb * N_PAIR + p] = st_new[p]


def _hgrn_scan(p_all, lb_logits, tables, layer):
    batch, n_tok, _ = p_all.shape
    n_bb = _batch_block(batch)
    n_chunk = n_tok // CHUNK
    head0 = COL_HEADS // D_HGRN
    chunk_of = _scan_chunk(n_chunk)
    _, hg_masks, _, hg_sums = tables
    per_dir = lambda shape: pl.BlockSpec((1,) + shape, lambda b, d, j: (d,) + (0,) * len(shape))
    tok = lambda col_of: pl.BlockSpec((n_bb, CHUNK, D_HGRN), lambda b, d, j: (b, chunk_of(d, j), col_of(d)))
    return pl.pallas_call(
        functools.partial(_hgrn_kernel, layer=layer, n_bb=n_bb),
        grid=(batch // n_bb, 2, n_chunk),
        in_specs=[
            tok(lambda d: head0 + 3), tok(lambda d: head0 + 4), tok(lambda d: head0 + 5 + d),
            per_dir(lb_logits.shape[1:]), per_dir(hg_masks.shape[1:]), per_dir(hg_sums.shape[1:]),
        ],
        out_specs=pl.BlockSpec((1, n_bb, CHUNK, D_HGRN), lambda b, d, j: (d, b, chunk_of(d, j), 0)),
        out_shape=jax.ShapeDtypeStruct((2, batch, n_tok, D_HGRN), _F32),
        scratch_shapes=[pltpu.VMEM((n_bb * N_PAIR, PAIR, PAIR), _F32)],
        compiler_params=pltpu.CompilerParams(
            dimension_semantics=("parallel", "arbitrary", "arbitrary"),
            vmem_limit_bytes=VMEM_LIMIT_BYTES),
        name="hgrn_scan",
    )(p_all, p_all, p_all, lb_logits, hg_masks, hg_sums)


def _readout_kernel(h_ref, gate_ref, pool_ref, yf_ref, yb_ref, bf_ref, bb_ref, of_ref, ob_ref,
                    gt_ref, pm_ref, pw_ref, ps_ref, gng_ref, gnb_ref, ng_ref, hm_ref,
                    wout_ref, lng_ref, lnb_ref, out_ref, *, alpha):
    pv = pool_ref[0]
    group = lax.broadcasted_iota(jnp.int32, pv.shape, 1) // POOL_GROUP
    pooled = jnp.zeros_like(pv)
    for g in range(len(POOL_WINDOWS)):
        pooled = pooled + jnp.where(group == g, _dot(pm_ref[0, g], pv), 0.0)
    mixed = _dot(pooled, pw_ref[...]) * ps_ref[...]

    y = yf_ref[0, 0] + yb_ref[0, 0]
    yc = y - _dot(y, hm_ref[...])
    var = _dot(yc * yc, hm_ref[...])
    rw = yc * lax.rsqrt(var + GN_EPS) * gng_ref[...] + gnb_ref[...] + bf_ref[0, 0] + bb_ref[0, 0]

    o = of_ref[0, 0] + ob_ref[0, 0]
    ho = o * lax.rsqrt(_dot(o * o, hm_ref[...]) + RMS_EPS) * ng_ref[...]

    gate = gate_ref[0]
    mix = jnp.concatenate([mixed, rw, ho], axis=1) * (gate * jax.nn.sigmoid(gate))
    proj = jnp.dot(mix.astype(jnp.bfloat16), wout_ref[...], preferred_element_type=_F32)
    out_ref[0] = _layer_norm(alpha * h_ref[0] + gt_ref[0] * proj) * lng_ref[...] + lnb_ref[...]


def _readout(h, p_all, y, bonus, o, gate_mod, params, alpha, skip_ctx):
    batch, n_tok, _ = h.shape
    n_blk = n_tok // TOK_BLOCK
    off = 1 if skip_ctx else 0
    tok = lambda width, col: pl.BlockSpec((1, TOK_BLOCK, width), lambda b, j: (b, j + off, col))
    dirs = lambda d: pl.BlockSpec((1, 1, TOK_BLOCK, D_RWKV), lambda b, j: (d, b, j + off, 0))
    full = lambda shape: pl.BlockSpec(shape, lambda b, j: (0,) * len(shape))
    n_win = len(POOL_WINDOWS)
    return pl.pallas_call(
        functools.partial(_readout_kernel, alpha=alpha),
        grid=(batch, n_blk - off),
        in_specs=[
            tok(D_MODEL, 0), tok(D_MODEL, COL_GATE // D_MODEL), tok(D_POOL, COL_POOL // D_POOL),
            dirs(0), dirs(1), dirs(0), dirs(1), dirs(0), dirs(1),
            pl.BlockSpec((1, 1, D_MODEL), lambda b, j: (jnp.where(j + off == 0, batch, b), 0, 0)),
            pl.BlockSpec((1, n_win, TOK_BLOCK, TOK_BLOCK),
                         lambda b, j: (jnp.where(j + off == 0, 0, 1), 0, 0, 0)),
            full((D_POOL, D_POOL)), full((1, D_POOL)),
            full((1, D_RWKV)), full((1, D_RWKV)), full((1, D_HGRN)), full((D_RWKV, D_RWKV)),
            full((D_MODEL, D_MODEL)), full((1, D_MODEL)), full((1, D_MODEL)),
        ],
        out_specs=pl.BlockSpec((1, TOK_BLOCK, D_MODEL), lambda b, j: (b, j, 0)),
        out_shape=jax.ShapeDtypeStruct((batch, n_tok - off * TOK_BLOCK, D_MODEL), _F32),
        compiler_params=pltpu.CompilerParams(
            dimension_semantics=("parallel", "parallel"), vmem_limit_bytes=VMEM_LIMIT_BYTES),
        name="readout",
    )(h, p_all, p_all, y, y, bonus, bonus, o, o, gate_mod,
      jnp.asarray(_pool_tables()), params["pool_w"], params["pool_scale"],
      params["gn_g"], params["gn_b"], params["norm_g"], _head_mean_table(),
      params["w_out"], params["ln_g"], params["ln_b"])


def _block_diag(blocks):
    n, a, b = blocks.shape
    out = jnp.zeros((n * a, n * b), blocks.dtype)
    for g in range(n):
        out = out.at[g * a:(g + 1) * a, g * b:(g + 1) * b].set(blocks[g])
    return out


def _layer_params(l, w_in, rwkv_shift, pool_w, pool_scale, rwkv_w0, rwkv_w_up, rwkv_a0, rwkv_a_up,
                  rwkv_k_k, rwkv_k_a, rwkv_r_k, rwkv_gn_g, rwkv_gn_b, hgrn_norm_g, w_out, ln_g, ln_b):
    sh = rwkv_shift[l]
    lora_sh = jnp.stack([
        jnp.concatenate([sh[:, 1152 + d * LORA:1152 + (d + 1) * LORA],
                         sh[:, 1280 + d * LORA:1280 + (d + 1) * LORA]], axis=1) for d in range(2)])
    pad = jnp.zeros((2, LORA, D_RWKV), _F32)
    return {
        "w_in": w_in[l][:, _column_perm()].astype(jnp.bfloat16),
        "shift_r": sh[None, :, 0:384], "shift_k": sh[None, :, 384:768], "shift_v": sh[None, :, 768:1152],
        "shift_lora": lora_sh,
        "w0": rwkv_w0[l][:, None, :], "a0": rwkv_a0[l][:, None, :],
        "w_up": jnp.concatenate([rwkv_w_up[l], pad], axis=1),
        "a_up": jnp.concatenate([pad, rwkv_a_up[l]], axis=1),
        "k_k": rwkv_k_k[l][None], "k_a": rwkv_k_a[l][None],
        "r_k": rwkv_r_k[l].reshape(2, 1, D_RWKV),
        "pool_w": _block_diag(pool_w[l]), "pool_scale": pool_scale[l][None],
        "gn_g": rwkv_gn_g[l][None], "gn_b": rwkv_gn_b[l][None], "norm_g": hgrn_norm_g[l][None],
        "w_out": w_out[l].astype(jnp.bfloat16), "ln_g": ln_g[l][None], "ln_b": ln_b[l][None],
    }


def kernel(x, c, ctx, c_ctx, mod_w, mod_b, w_in, rwkv_shift, pool_w, pool_scale, rwkv_w0, rwkv_w_up,
           rwkv_a0, rwkv_a_up, rwkv_k_k, rwkv_k_a, rwkv_r_k, rwkv_gn_g, rwkv_gn_b, hgrn_lb_logits,
           hgrn_norm_g, w_out, ln_g, ln_b):
    depth = mod_w.shape[0]
    batch = x.shape[0]
    assert ctx.shape[1] == CTX_LEN == TOK_BLOCK and x.shape[1] % TOK_BLOCK == 0
    alpha = float((2 * depth) ** 0.25)
    tables = tuple(jnp.asarray(t) for t in _scan_tables())

    c_all = jnp.concatenate([c.astype(_F32), c_ctx.astype(_F32)[None]], axis=0)
    mod = _modulation(c_all, mod_w, mod_b)
    h = jnp.concatenate([ctx.astype(_F32), x.astype(_F32)], axis=1)

    for l in range(depth):
        prm = _layer_params(l, w_in, rwkv_shift, pool_w, pool_scale, rwkv_w0, rwkv_w_up, rwkv_a0,
                            rwkv_a_up, rwkv_k_k, rwkv_k_a, rwkv_r_k, rwkv_gn_g, rwkv_gn_b,
                            hgrn_norm_g, w_out, ln_g, ln_b)
        shift = mod[l, :, None, 0:D_MODEL]
        scale = mod[l, :, None, D_MODEL:2 * D_MODEL]
        gate_mod = mod[l, :, None, 2 * D_MODEL:]
        p_all = _input_projection(h, shift, scale, prm["w_in"])
        y, bonus = _rwkv_scan(p_all, prm, tables)
        o = _hgrn_scan(p_all, hgrn_lb_logits, tables, l)
        h = _readout(h, p_all, y, bonus, o, gate_mod, prm, alpha, skip_ctx=(l == depth - 1))
    return h.astype(x.dtype)
```

```python
import functools
import math

import numpy as np
import jax
import jax.numpy as jnp
from jax import lax
from jax.experimental import pallas as pl
from jax.experimental.pallas import tpu as pltpu

D_MODEL = 1024
CTX_LEN = 256
GRID_W = 64
HEAD_DIM = 64
D_POOL = 256
D_RWKV = 384
D_HGRN = 384
POOL_WINDOWS = (2, 4, 8, 16)
POOL_GROUP = 64
LORA = 64
D_IN = 4224
LN_EPS = 1e-5
GN_EPS = 64e-5
RMS_EPS = 1e-6

TOK_BLOCK = 256
CHUNK = 64
PAIR = 2 * HEAD_DIM
N_PAIR = D_RWKV // PAIR
BATCH_BLOCK = 8
PREP_GROUP = 4
WAVE_SKEW = 1
CHUNK_LEVELS = (1, 2, 4, 8, 16, 32)
HALO = 8
VMEM_LIMIT_BYTES = 56 * 1024 * 1024

COL_GATE = 0
COL_POOL = 1024
COL_LORA = 1280
COL_HEADS = 1536

_F32 = jnp.float32
_HI = lax.Precision.HIGHEST


def _softplus(x):
    return jnp.maximum(x, 0.0) + jnp.log1p(jnp.exp(-jnp.abs(x)))


def _dot_exact(a, b):
    return jnp.dot(a, b, precision=_HI, preferred_element_type=_F32)


def _bf16(x):
    return x.astype(jnp.bfloat16)


def _dot(a, b):
    return jnp.dot(_bf16(a), _bf16(b), preferred_element_type=_F32)


def _dot_split(table2, x):
    head = _bf16(x)
    rest = _bf16(x - head.astype(_F32))
    return jnp.dot(_bf16(table2), jnp.concatenate([head, rest], axis=0), preferred_element_type=_F32)


def _dot_nt(a, b):
    return lax.dot_general(_bf16(a), _bf16(b), (((1,), (1,)), ((), ())), preferred_element_type=_F32)


def _dot_tn(a, b):
    return lax.dot_general(_bf16(a), _bf16(b), (((0,), (0,)), ((), ())), preferred_element_type=_F32)


def _chunk_pos(direction):
    t = np.arange(CHUNK)
    return t if direction == 0 else CHUNK - 1 - t


def _level_tables(direction):
    pos = _chunk_pos(direction)
    masks, sums = [], []
    for m in CHUNK_LEVELS:
        blk = pos // m
        pair = pos // (2 * m)
        late = (blk % 2) == 1
        same = pair[:, None] == pair[None, :]
        masks.append(same & late[:, None] & (~late)[None, :])
        late_rows = same & late[:, None] & late[None, :] & (pos[None, :] <= pos[:, None])
        early_rows = same & (~late)[:, None] & (~late)[None, :] & (pos[None, :] > pos[:, None])
        sums.append(late_rows | early_rows)
    return masks, sums


def _tile2(m):
    return np.tile(m.astype(np.float32), (1, 2))


@functools.lru_cache(maxsize=None)
def _scan_tables():
    rw_masks, hg_masks, cums, hg_sums = [], [], [], []
    for d in range(2):
        pos = _chunk_pos(d)
        strict = pos[None, :] < pos[:, None]
        incl = pos[None, :] <= pos[:, None]
        lv_masks, lv_sums = _level_tables(d)
        eye = np.eye(CHUNK, dtype=bool)
        rw_masks.append(np.stack([_tile2(strict), _tile2(incl)] + [_tile2(m) for m in lv_masks]
                                 + [-_tile2(strict)]))
        hg_masks.append(np.stack([_tile2(m) for m in lv_masks] + [_tile2(eye)]))
        cums.append(np.tile(incl.astype(np.float32), (1, 2)))
        hg_sums.append(np.tile(np.concatenate([incl] + lv_sums, axis=0).astype(np.float32), (1, 2)))
    return (np.stack(rw_masks), np.stack(hg_masks), np.stack(cums), np.stack(hg_sums))


@functools.lru_cache(maxsize=None)
def _pool_tables():
    out = np.zeros((2, len(POOL_WINDOWS), TOK_BLOCK, TOK_BLOCK), np.float32)
    for kind, seg in enumerate((TOK_BLOCK, GRID_W)):
        for g, win in enumerate(POOL_WINDOWS):
            left = win // 2
            right = win - 1 - left
            for t in range(TOK_BLOCK):
                base = (t // seg) * seg
                lo = max(t - left, base)
                hi = min(t + right, base + seg - 1) + 1
                out[kind, g, t, lo:hi] = 1.0 / (hi - lo)
                out[kind, g, t, t] -= 1.0
    return out


@functools.lru_cache(maxsize=None)
def _head_mean_table():
    h = np.arange(D_RWKV) // HEAD_DIM
    return (h[:, None] == h[None, :]).astype(np.float32) / HEAD_DIM


@functools.lru_cache(maxsize=None)
def _column_perm():
    pool = np.arange(0, 256)
    u0 = 256
    r = u0 + np.arange(0, 384)
    k = u0 + np.arange(384, 768)
    v = u0 + np.arange(768, 1152)
    lora = []
    for d in range(2):
        lora.append(u0 + 1152 + d * LORA + np.arange(LORA))
        lora.append(u0 + 1280 + d * LORA + np.arange(LORA))
    f0 = 256 + 1408
    q = f0 + np.arange(0, 384)
    i = f0 + np.arange(384, 768)
    z0 = f0 + np.arange(768, 1152)
    z1 = f0 + np.arange(1152, 1536)
    gate = np.arange(3200, 4224)
    perm = np.concatenate([gate, pool] + lora + [r, k, v, q, i, z0, z1])
    assert perm.shape[0] == D_IN and np.unique(perm).shape[0] == D_IN
    return perm


def _mod_kernel(c_ref, w_ref, b_ref, o_ref):
    c = c_ref[...]
    o_ref[0] = _dot_exact(c * jax.nn.sigmoid(c), w_ref[0]) + b_ref[0]


def _modulation(c_all, mod_w, mod_b):
    depth = mod_w.shape[0]
    rows = c_all.shape[0]
    n_tile = 1024
    return pl.pallas_call(
        _mod_kernel,
        grid=(depth, 3 * D_MODEL // n_tile),
        in_specs=[
            pl.BlockSpec((rows, D_MODEL), lambda l, n: (0, 0)),
            pl.BlockSpec((1, D_MODEL, n_tile), lambda l, n: (l, 0, n)),
            pl.BlockSpec((1, 1, n_tile), lambda l, n: (l, 0, n)),
        ],
        out_specs=pl.BlockSpec((1, rows, n_tile), lambda l, n: (l, 0, n)),
        out_shape=jax.ShapeDtypeStruct((depth, rows, 3 * D_MODEL), _F32),
        compiler_params=pltpu.CompilerParams(
            dimension_semantics=("arbitrary", "arbitrary"), vmem_limit_bytes=VMEM_LIMIT_BYTES),
        name="modulation",
    )(c_all, mod_w, mod_b.reshape(depth, 1, 3 * D_MODEL))


def _layer_norm(x):
    mu = jnp.mean(x, axis=-1, keepdims=True)
    xc = x - mu
    var = jnp.mean(xc * xc, axis=-1, keepdims=True)
    return xc * lax.rsqrt(var + LN_EPS)


def _proj_kernel(h_ref, sh_ref, sc_ref, w_ref, p_ref):
    xn = _layer_norm(h_ref[0]) * (1.0 + sc_ref[0]) + sh_ref[0]
    p_ref[0] = jnp.dot(xn.astype(jnp.bfloat16), w_ref[...], preferred_element_type=_F32)


def _mod_row(batch):
    return lambda b, j: (jnp.where(j == 0, batch, b), 0, 0)


def _input_projection(h, shift, scale, w_in_bf16):
    batch, n_tok, _ = h.shape
    return pl.pallas_call(
        _proj_kernel,
        grid=(batch, n_tok // TOK_BLOCK),
        in_specs=[
            pl.BlockSpec((1, TOK_BLOCK, D_MODEL), lambda b, j: (b, j, 0)),
            pl.BlockSpec((1, 1, D_MODEL), _mod_row(batch)),
            pl.BlockSpec((1, 1, D_MODEL), _mod_row(batch)),
            pl.BlockSpec((D_MODEL, D_IN), lambda b, j: (0, 0)),
        ],
        out_specs=pl.BlockSpec((1, TOK_BLOCK, D_IN), lambda b, j: (b, j, 0)),
        out_shape=jax.ShapeDtypeStruct((batch, n_tok, D_IN), _F32),
        compiler_params=pltpu.CompilerParams(
            dimension_semantics=("parallel", "parallel"), vmem_limit_bytes=VMEM_LIMIT_BYTES),
        name="input_projection",
    )(h, shift, scale, w_in_bf16)


def _scan_chunk(n_chunk):
    n_ctx = CTX_LEN // CHUNK
    return lambda d, j: jnp.where(d == 0, j, jnp.where(j < n_ctx, n_ctx - 1 - j, n_chunk + n_ctx - 1 - j))


def _batch_block(batch):
    return math.gcd(batch, BATCH_BLOCK)


def _block_diag_pair(x16, lo16, hi16):
    return jnp.concatenate([x16 * lo16, x16 * hi16], axis=0)


def _pair_masks():
    lane = lax.broadcasted_iota(jnp.int32, (CHUNK, PAIR), 1)
    lo16 = (lane < HEAD_DIM).astype(jnp.bfloat16)
    row = lax.broadcasted_iota(jnp.int32, (PAIR, PAIR), 0)
    col = lax.broadcasted_iota(jnp.int32, (PAIR, PAIR), 1)
    same_head = ((row < HEAD_DIM) == (col < HEAD_DIM)).astype(_F32)
    return lo16, 1.0 - lo16, same_head


def _run_wavefront(programs):
    live = [True] * len(programs)
    rnd = 0
    while any(live):
        for g in reversed(range(len(programs))):
            if live[g] and rnd >= WAVE_SKEW * g:
                try:
                    next(programs[g])
                except StopIteration:
                    live[g] = False
        rnd += 1


def _rwkv_programs(r_ref, k_ref, v_ref, lo_ref,
                   rp_ref, kp_ref, vp_ref, lp_ref, rn_ref, kn_ref, vn_ref, ln_ref,
                   shr_ref, shk_ref, shv_ref, shl_ref,
                   w0_ref, wup_ref, a0_ref, aup_ref, kk_ref, ka_ref, rk_ref,
                   hm_ref, mask_ref, cum_ref,
                   y_ref, bonus_ref, state_ref, *, n_chunk, n_bb):
    d = pl.program_id(1)
    j = pl.program_id(2)
    chunk = _scan_chunk(n_chunk)(d, j)
    n_ctx = CTX_LEN // CHUNK
    prev_ok = jnp.where((chunk == 0) | (chunk == n_ctx), 0.0, 1.0).astype(_F32)
    next_ok = jnp.where((chunk == n_ctx - 1) | (chunk == n_chunk - 1), 0.0, 1.0).astype(_F32)

    @pl.when(j == 0)
    def _():
        state_ref[...] = jnp.zeros_like(state_ref)

    def shifted(main_ref, prev_ref, next_ref, w_ref, bbs):
        w = w_ref[0]
        outs = []
        for bb in bbs:
            x = main_ref[bb]
            row = lax.broadcasted_iota(jnp.int32, x.shape, 0)
            prev_row = prev_ref[bb, HALO - 1:HALO, :] * prev_ok
            next_row = next_ref[bb, 0:1, :] * next_ok
            x_prev = jnp.where(row == 0, prev_row, pltpu.roll(x, 1, axis=0))
            x_next = jnp.where(row == CHUNK - 1, next_row, pltpu.roll(x, CHUNK - 1, axis=0))
            outs.append(w[0:1] * x_prev + w[1:2] * x + w[2:3] * x_next)
        return jnp.concatenate(outs, axis=0)

    group_size = math.gcd(n_bb, PREP_GROUP)
    groups = {}

    def group_prep(g):
        if g in groups:
            return groups[g]
        bbs = range(g * group_size, (g + 1) * group_size)
        r = shifted(r_ref, rp_ref, rn_ref, shr_ref, bbs)
        k = shifted(k_ref, kp_ref, kn_ref, shk_ref, bbs)
        v = shifted(v_ref, vp_ref, vn_ref, shv_ref, bbs)
        lora = shifted(lo_ref, lp_ref, ln_ref, shl_ref, bbs)
        w_pre = w0_ref[0] + _dot(jnp.tanh(lora), wup_ref[0])
        w_log = -_softplus(-w_pre) - 0.5
        lw = -jnp.exp(w_log)
        a = jax.nn.sigmoid(a0_ref[0] + _dot(lora, aup_ref[0]))
        kk = k * kk_ref[...]
        kk = kk * lax.rsqrt(_dot(kk * kk, hm_ref[...]) * HEAD_DIM + 1e-12)
        k_mod = k * (1.0 + (a - 1.0) * ka_ref[...])
        bonus = _dot(r * k_mod * rk_ref[0], hm_ref[...]) * HEAD_DIM * v
        for i, bb in enumerate(bbs):
            bonus_ref[0, bb] = bonus[i * CHUNK:(i + 1) * CHUNK]
        groups[g] = dict(r=r, v=v, lw=lw, kk=kk, k_mod=k_mod, beta=a * kk)
        return groups[g]

    lo16, hi16, same_head = _pair_masks()
    bd = lambda x16: _block_diag_pair(x16, lo16, hi16)
    row_c = lax.broadcasted_iota(jnp.int32, (CHUNK, PAIR), 0)
    lane_c = lax.broadcasted_iota(jnp.int32, (CHUNK, PAIR), 1)
    eye_ls = ((lane_c == row_c) | (lane_c == row_c + HEAD_DIM)).astype(_F32)
    eye = (lax.broadcasted_iota(jnp.int32, (PAIR, PAIR), 0)
           == lax.broadcasted_iota(jnp.int32, (PAIR, PAIR), 1)).astype(_F32)
    zeros16 = jnp.zeros((PAIR, PAIR), jnp.bfloat16)
    zeros_c16 = jnp.zeros((CHUNK, PAIR), jnp.bfloat16)
    m_strict = mask_ref[0, 0]
    m_incl2 = jnp.concatenate([mask_ref[0, 1], mask_ref[0, 1]], axis=1)
    m_neg_strict = mask_ref[0, 2 + len(CHUNK_LEVELS)]
    lv_masks = [mask_ref[0, 2 + lv] for lv in range(len(CHUNK_LEVELS))]
    cum = cum_ref[0]
    lanes = lambda p: slice(p * PAIR, (p + 1) * PAIR)
    pairs = range(N_PAIR)
    results = []

    def program(bb):
        tok = group_prep(bb // group_size)
        rows = slice((bb % group_size) * CHUNK, (bb % group_size + 1) * CHUNK)
        lw_c = tok["lw"][rows]
        g_incl = _dot_split(cum, lw_c)
        yield
        g_excl = g_incl - lw_c
        g_mid = g_incl[CHUNK // 2:CHUNK // 2 + 1, :]
        g_end = jnp.sum(lw_c, axis=0, keepdims=True)
        e_in = jnp.exp(g_incl - g_mid)
        e_out = jnp.exp(g_mid - g_incl)
        e_end = jnp.exp(g_end - g_incl)
        p_end = jnp.exp(g_end)
        p_mid = jnp.exp(g_mid)
        kkd = tok["kk"][rows] * jnp.exp(g_excl - g_mid)
        rd = tok["r"][rows] * e_in
        k_c, b_c = tok["k_mod"][rows], tok["beta"][rows]
        ch = {name: _bf16(val) for name, val in dict(
            kkd=kkd, kkd_n=kkd * (-p_mid), rd=rd, ki=k_c * e_out, bi=b_c * e_out,
            kd=k_c * e_end, bd=b_c * e_end, v=tok["v"][rows]).items()}
        nat = lambda p, name: ch[name][:, lanes(p)]
        rd_abs = rd * p_mid

        gm = [_dot_nt(jnp.concatenate([nat(p, "kkd"), nat(p, "rd")], axis=0),
                      jnp.concatenate([bd(nat(p, "ki")), bd(nat(p, "bi"))], axis=0))
              for p in pairs]
        yield
        g16 = [_bf16(g) for g in gm]
        a_kn = [g[:CHUNK, :PAIR] * m_neg_strict for g in g16]
        a_b = [g[:CHUNK, PAIR:] * m_strict for g in g16]
        r_kb = [g[CHUNK:, :] * m_incl2 for g in g16]
        v_bd = [bd(nat(p, "v")) for p in pairs]
        t_inv = [eye_ls - (a_b[p] * lv_masks[0]).astype(_F32) for p in pairs]
        w0n = [_dot(a_kn[p], v_bd[p]) for p in pairs]
        for lv in range(1, len(CHUNK_LEVELS)):
            t16 = [_bf16(t) for t in t_inv]
            x = [_dot(t16[p], bd(a_b[p] * lv_masks[lv])) for p in pairs]
            yield
            t_new = [_dot(x[p], bd(t16[p])) for p in pairs]
            yield
            t_inv = [t_inv[p] - t_new[p] for p in pairs]
        twk = [_dot(t_inv[p], jnp.concatenate([bd(_bf16(w0n[p])), bd(nat(p, "kkd_n"))], axis=1)) for p in pairs]
        yield
        t16 = [_bf16(t) for t in twk]
        z = [jnp.concatenate([jnp.concatenate([v_bd[p], zeros16], axis=1),
                              jnp.concatenate([bd(t16[p][:, :PAIR]), bd(t16[p][:, PAIR:])], axis=1)], axis=0)
             for p in pairs]
        yr = [_dot(r_kb[p], z[p]) for p in pairs]
        mn = [_dot_tn(jnp.concatenate([nat(p, "kd"), nat(p, "bd")], axis=0),
                      jnp.concatenate([jnp.concatenate([nat(p, "v"), zeros_c16], axis=1), t16[p]], axis=0))
              for p in pairs]
        yield
        st = [state_ref[bb * N_PAIR + p] for p in pairs]
        y_new = [yr[p][:, :PAIR] + _dot(rd_abs[:, lanes(p)] + yr[p][:, PAIR:], st[p]) for p in pairs]
        st_new = [_dot(mn[p][:, PAIR:] * same_head + eye * p_end[:, lanes(p)], st[p])
                  + mn[p][:, :PAIR] * same_head for p in pairs]
        results.append((bb, y_new, st_new))

    def finish():
        for bb, y_new, st_new in results:
            for p in pairs:
                y_ref[0, bb, :, lanes(p)] = y_new[p]
                state_ref[bb * N_PAIR + p] = st_new[p]

    return [program(bb) for bb in range(n_bb)], finish


def _hgrn_programs(q_ref, i_ref, z_ref, lb_ref, mask_ref, sum_ref, o_ref, state_ref, *, layer, n_bb):
    j = pl.program_id(2)

    @pl.when(j == 0)
    def _():
        state_ref[...] = jnp.zeros_like(state_ref)

    logits = lb_ref[0]
    e = jnp.exp(logits - jnp.max(logits, axis=0, keepdims=True))
    lb_w = e / jnp.sum(e, axis=0, keepdims=True)
    csum = lb_w[0:1]
    for l in range(1, layer + 1):
        csum = csum + lb_w[l:l + 1]
    lb = jnp.maximum(csum - lb_w[0:1], 0.0)
    log_lb = jnp.log(lb)
    log_1m_lb = jnp.log1p(-lb)

    lo16, hi16, same_head = _pair_masks()
    bd = lambda x16: _block_diag_pair(x16, lo16, hi16)
    n_lv = len(CHUNK_LEVELS)
    lv_mask = [mask_ref[0, n_lv]] + [mask_ref[0, lv] for lv in range(n_lv)]
    lanes = lambda p: slice(p * PAIR, (p + 1) * PAIR)
    sum_table = sum_ref[0]
    pairs = range(N_PAIR)
    results = []

    def program(bb):
        z = z_ref[bb]
        log_rest = log_1m_lb - _softplus(-z)
        logf = jnp.maximum(log_lb, log_rest) + jnp.log1p(jnp.exp(-jnp.abs(log_lb - log_rest)))
        k_c = (1.0 - lb) * jax.nn.sigmoid(-z)
        q_c = q_ref[bb]
        i_c = i_ref[bb]
        sums = _dot_split(sum_table, logf)
        yield
        g_incl = sums[:CHUNK]
        g_end = jnp.min(g_incl, axis=0, keepdims=True)
        p_end = jnp.exp(g_end)
        qd = q_c * jnp.exp(g_incl)
        kd = k_c * jnp.exp(g_end - g_incl)
        st = [state_ref[bb * N_PAIR + p] for p in pairs]
        inter = [_dot_nt(qd[:, lanes(p)], st[p]) for p in pairs]
        outer = [_dot_tn(i_c[:, lanes(p)], kd[:, lanes(p)]) for p in pairs]
        lv_q, lv_k = [_bf16(q_c)], [_bf16(k_c)]
        for lv in range(n_lv):
            x = jnp.exp(sums[(1 + lv) * CHUNK:(2 + lv) * CHUNK])
            lv_q.append(_bf16(q_c * x))
            lv_k.append(_bf16(k_c * x))
        i16 = _bf16(i_c)
        scores = []
        for p in pairs:
            parts = [_dot_nt(q_l[:, lanes(p)], bd(k_l[:, lanes(p)])) for q_l, k_l in zip(lv_q, lv_k)]
            yield
            sc = parts[0] * lv_mask[0]
            for part, m_l in zip(parts[1:], lv_mask[1:]):
                sc = sc + part * m_l
            scores.append(sc)
        intra = [_dot(scores[p], bd(i16[:, lanes(p)])) for p in pairs]
        st_new = [st[p] * p_end[:, lanes(p)] + outer[p] * same_head for p in pairs]
        yield
        results.append((bb, [intra[p] + inter[p] for p in pairs], st_new))

    def finish():
        for bb, o_new, st_new in results:
            for p in pairs:
                o_ref[0, bb, :, lanes(p)] = o_new[p]
                state_ref[bb * N_PAIR + p] = st_new[p]

    return [program(bb) for bb in range(n_bb)], finish


N_RWKV_IN = 26
N_HGRN_IN = 6


def _scan_kernel(*refs, n_chunk, n_bb, layer):
    rw_in = refs[:N_RWKV_IN]
    hg_in = refs[N_RWKV_IN:N_RWKV_IN + N_HGRN_IN]
    y_ref, bonus_ref, o_ref, rw_state, hg_state = refs[N_RWKV_IN + N_HGRN_IN:]
    rw_programs, rw_finish = _rwkv_programs(*rw_in, y_ref, bonus_ref, rw_state, n_chunk=n_chunk, n_bb=n_bb)
    hg_programs, hg_finish = _hgrn_programs(*hg_in, o_ref, hg_state, layer=layer, n_bb=n_bb)
    programs = [prog for pair in zip(rw_programs, hg_programs) for prog in pair]
    _run_wavefront(programs)
    rw_finish()
    hg_finish()


def _mixer_scan(p_all, params, lb_logits, tables, layer):
    batch, n_tok, _ = p_all.shape
    n_bb = _batch_block(batch)
    n_chunk = n_tok // CHUNK
    n_halo = CHUNK // HALO
    head0 = COL_HEADS // D_RWKV
    lora0 = COL_LORA // PAIR
    chunk_of = _scan_chunk(n_chunk)

    def main(col_of, width=D_RWKV):
        return pl.BlockSpec((n_bb, CHUNK, width), lambda b, d, j: (b, chunk_of(d, j), col_of(d)))

    def prev(col_of, width=D_RWKV):
        return pl.BlockSpec(
            (n_bb, HALO, width),
            lambda b, d, j: (b, jnp.maximum(chunk_of(d, j) * n_halo - 1, 0), col_of(d)))

    def nxt(col_of, width=D_RWKV):
        return pl.BlockSpec(
            (n_bb, HALO, width),
            lambda b, d, j: (b, jnp.minimum((chunk_of(d, j) + 1) * n_halo, n_tok // HALO - 1), col_of(d)))

    col = lambda c: (lambda d: c)
    cols = [col(head0), col(head0 + 1), col(head0 + 2)]
    lora_col = lambda d: lora0 + d
    full = lambda shape: pl.BlockSpec(shape, lambda b, d, j: (0,) * len(shape))
    per_dir = lambda shape: pl.BlockSpec((1,) + shape, lambda b, d, j: (d,) + (0,) * len(shape))
    out_spec = pl.BlockSpec((1, n_bb, CHUNK, D_RWKV), lambda b, d, j: (d, b, chunk_of(d, j), 0))

    rw_masks, hg_masks, cums, hg_sums = tables
    rw_masks = rw_masks.astype(jnp.bfloat16)
    rwkv_specs = (
        [main(c) for c in cols] + [main(lora_col, PAIR)]
        + [prev(c) for c in cols] + [prev(lora_col, PAIR)]
        + [nxt(c) for c in cols] + [nxt(lora_col, PAIR)]
        + [
            full((1, 3, D_RWKV)), full((1, 3, D_RWKV)), full((1, 3, D_RWKV)), per_dir((3, PAIR)),
            per_dir((1, D_RWKV)), per_dir((PAIR, D_RWKV)), per_dir((1, D_RWKV)), per_dir((PAIR, D_RWKV)),
            full((1, D_RWKV)), full((1, D_RWKV)), per_dir((1, D_RWKV)),
            full((D_RWKV, D_RWKV)), per_dir(rw_masks.shape[1:]), per_dir(cums.shape[1:]),
        ])
    hgrn_specs = [
        main(col(head0 + 3)), main(col(head0 + 4)), main(lambda d: head0 + 5 + d),
        per_dir(lb_logits.shape[1:]), per_dir(hg_masks.shape[1:]), per_dir(hg_sums.shape[1:]),
    ]
    assert len(rwkv_specs) == N_RWKV_IN and len(hgrn_specs) == N_HGRN_IN
    out_struct = jax.ShapeDtypeStruct((2, batch, n_tok, D_RWKV), _F32)
    return pl.pallas_call(
        functools.partial(_scan_kernel, n_chunk=n_chunk, n_bb=n_bb, layer=layer),
        grid=(batch // n_bb, 2, n_chunk),
        in_specs=rwkv_specs + hgrn_specs,
        out_specs=[out_spec, out_spec, out_spec],
        out_shape=[out_struct, out_struct, out_struct],
        scratch_shapes=[pltpu.VMEM((n_bb * N_PAIR, PAIR, PAIR), _F32),
                        pltpu.VMEM((n_bb * N_PAIR, PAIR, PAIR), _F32)],
        compiler_params=pltpu.CompilerParams(
            dimension_semantics=("parallel", "arbitrary", "arbitrary"),
            vmem_limit_bytes=VMEM_LIMIT_BYTES),
        name="mixer_scan",
    )(p_all, p_all, p_all, p_all, p_all, p_all, p_all, p_all, p_all, p_all, p_all, p_all,
      params["shift_r"], params["shift_k"], params["shift_v"], params["shift_lora"],
      params["w0"], params["w_up"], params["a0"], params["a_up"],
      params["k_k"], params["k_a"], params["r_k"],
      _head_mean_table(), rw_masks, cums,
      p_all, p_all, p_all, lb_logits, hg_masks, hg_sums)


def _readout_kernel(h_ref, gate_ref, pool_ref, yf_ref, yb_ref, bf_ref, bb_ref, of_ref, ob_ref,
                    gt_ref, pm_ref, pw_ref, ps_ref, gng_ref, gnb_ref, ng_ref, hm_ref,
                    wout_ref, lng_ref, lnb_ref, out_ref, *, alpha):
    pv = pool_ref[0]
    group = lax.broadcasted_iota(jnp.int32, pv.shape, 1) // POOL_GROUP
    pooled = jnp.zeros_like(pv)
    for g in range(len(POOL_WINDOWS)):
        pooled = pooled + jnp.where(group == g, _dot(pm_ref[0, g], pv), 0.0)
    mixed = _dot(pooled, pw_ref[...]) * ps_ref[...]

    y = yf_ref[0, 0] + yb_ref[0, 0]
    yc = y - _dot(y, hm_ref[...])
    var = _dot(yc * yc, hm_ref[...])
    rw = yc * lax.rsqrt(var + GN_EPS) * gng_ref[...] + gnb_ref[...] + bf_ref[0, 0] + bb_ref[0, 0]

    o = of_ref[0, 0] + ob_ref[0, 0]
    ho = o * lax.rsqrt(_dot(o * o, hm_ref[...]) + RMS_EPS) * ng_ref[...]

    gate = gate_ref[0]
    mix = jnp.concatenate([mixed, rw, ho], axis=1) * (gate * jax.nn.sigmoid(gate))
    proj = jnp.dot(mix.astype(jnp.bfloat16), wout_ref[...], preferred_element_type=_F32)
    out_ref[0] = _layer_norm(alpha * h_ref[0] + gt_ref[0] * proj) * lng_ref[...] + lnb_ref[...]


def _readout(h, p_all, y, bonus, o, gate_mod, params, alpha, skip_ctx):
    batch, n_tok, _ = h.shape
    n_blk = n_tok // TOK_BLOCK
    off = 1 if skip_ctx else 0
    tok = lambda width, col: pl.BlockSpec((1, TOK_BLOCK, width), lambda b, j: (b, j + off, col))
    dirs = lambda d: pl.BlockSpec((1, 1, TOK_BLOCK, D_RWKV), lambda b, j: (d, b, j + off, 0))
    full = lambda shape: pl.BlockSpec(shape, lambda b, j: (0,) * len(shape))
    n_win = len(POOL_WINDOWS)
    return pl.pallas_call(
        functools.partial(_readout_kernel, alpha=alpha),
        grid=(batch, n_blk - off),
        in_specs=[
            tok(D_MODEL, 0), tok(D_MODEL, COL_GATE // D_MODEL), tok(D_POOL, COL_POOL // D_POOL),
            dirs(0), dirs(1), dirs(0), dirs(1), dirs(0), dirs(1),
            pl.BlockSpec((1, 1, D_MODEL), lambda b, j: (jnp.where(j + off == 0, batch, b), 0, 0)),
            pl.BlockSpec((1, n_win, TOK_BLOCK, TOK_BLOCK),
                         lambda b, j: (jnp.where(j + off == 0, 0, 1), 0, 0, 0)),
            full((D_POOL, D_POOL)), full((1, D_POOL)),
            full((1, D_RWKV)), full((1, D_RWKV)), full((1, D_HGRN)), full((D_RWKV, D_RWKV)),
            full((D_MODEL, D_MODEL)), full((1, D_MODEL)), full((1, D_MODEL)),
        ],
        out_specs=pl.BlockSpec((1, TOK_BLOCK, D_MODEL), lambda b, j: (b, j, 0)),
        out_shape=jax.ShapeDtypeStruct((batch, n_tok - off * TOK_BLOCK, D_MODEL), _F32),
        compiler_params=pltpu.CompilerParams(
            dimension_semantics=("parallel", "parallel"), vmem_limit_bytes=VMEM_LIMIT_BYTES),
        name="readout",
    )(h, p_all, p_all, y, y, bonus, bonus, o, o, gate_mod,
      jnp.asarray(_pool_tables()), params["pool_w"], params["pool_scale"],
      params["gn_g"], params["gn_b"], params["norm_g"], _head_mean_table(),
      params["w_out"], params["ln_g"], params["ln_b"])


def _block_diag(blocks):
    n, a, b = blocks.shape
    out = jnp.zeros((n * a, n * b), blocks.dtype)
    for g in range(n):
        out = out.at[g * a:(g + 1) * a, g * b:(g + 1) * b].set(blocks[g])
    return out


def _permute_columns(w):
    perm = _column_perm()
    cuts = [0] + [i for i in range(1, len(perm)) if perm[i] != perm[i - 1] + 1] + [len(perm)]
    return jnp.concatenate([w[:, int(perm[a]):int(perm[b - 1]) + 1] for a, b in zip(cuts[:-1], cuts[1:])], axis=1)


def _layer_params(l, w_in, rwkv_shift, pool_w, pool_scale, rwkv_w0, rwkv_w_up, rwkv_a0, rwkv_a_up,
                  rwkv_k_k, rwkv_k_a, rwkv_r_k, rwkv_gn_g, rwkv_gn_b, hgrn_norm_g, w_out, ln_g, ln_b):
    sh = rwkv_shift[l]
    lora_sh = jnp.stack([
        jnp.concatenate([sh[:, 1152 + d * LORA:1152 + (d + 1) * LORA],
                         sh[:, 1280 + d * LORA:1280 + (d + 1) * LORA]], axis=1) for d in range(2)])
    pad = jnp.zeros((2, LORA, D_RWKV), _F32)
    return {
        "w_in": _permute_columns(w_in[l]).astype(jnp.bfloat16),
        "shift_r": sh[None, :, 0:384], "shift_k": sh[None, :, 384:768], "shift_v": sh[None, :, 768:1152],
        "shift_lora": lora_sh,
        "w0": rwkv_w0[l][:, None, :], "a0": rwkv_a0[l][:, None, :],
        "w_up": jnp.concatenate([rwkv_w_up[l], pad], axis=1),
        "a_up": jnp.concatenate([pad, rwkv_a_up[l]], axis=1),
        "k_k": rwkv_k_k[l][None], "k_a": rwkv_k_a[l][None],
        "r_k": rwkv_r_k[l].reshape(2, 1, D_RWKV),
        "pool_w": _block_diag(pool_w[l]), "pool_scale": pool_scale[l][None],
        "gn_g": rwkv_gn_g[l][None], "gn_b": rwkv_gn_b[l][None], "norm_g": hgrn_norm_g[l][None],
        "w_out": w_out[l].astype(jnp.bfloat16), "ln_g": ln_g[l][None], "ln_b": ln_b[l][None],
    }


def kernel(x, c, ctx, c_ctx, mod_w, mod_b, w_in, rwkv_shift, pool_w, pool_scale, rwkv_w0, rwkv_w_up,
           rwkv_a0, rwkv_a_up, rwkv_k_k, rwkv_k_a, rwkv_r_k, rwkv_gn_g, rwkv_gn_b, hgrn_lb_logits,
           hgrn_norm_g, w_out, ln_g, ln_b):
    depth = mod_w.shape[0]
    batch = x.shape[0]
    assert ctx.shape[1] == CTX_LEN == TOK_BLOCK and x.shape[1] % TOK_BLOCK == 0
    alpha = float((2 * depth) ** 0.25)
    tables = tuple(jnp.asarray(t) for t in _scan_tables())

    c_all = jnp.concatenate([c.astype(_F32), c_ctx.astype(_F32)[None]], axis=0)
    mod = _modulation(c_all, mod_w, mod_b)
    h = jnp.concatenate([ctx.astype(_F32), x.astype(_F32)], axis=1)

    for l in range(depth):
        prm = _layer_params(l, w_in, rwkv_shift, pool_w, pool_scale, rwkv_w0, rwkv_w_up, rwkv_a0,
                            rwkv_a_up, rwkv_k_k, rwkv_k_a, rwkv_r_k, rwkv_gn_g, rwkv_gn_b,
                            hgrn_norm_g, w_out, ln_g, ln_b)
        shift = mod[l, :, None, 0:D_MODEL]
        scale = mod[l, :, None, D_MODEL:2 * D_MODEL]
        gate_mod = mod[l, :, None, 2 * D_MODEL:]
        p_all = _input_projection(h, shift, scale, prm["w_in"])
        y, bonus, o = _mixer_scan(p_all, prm, hgrn_lb_logits, tables, l)
        h = _readout(h, p_all, y, bonus, o, gate_mod, prm, alpha, skip_ctx=(l == depth - 1))
    return h.astype(x.dtype)
```

```python
import functools
import math

import numpy as np
import jax
import jax.numpy as jnp
from jax import lax
from jax.experimental import pallas as pl
from jax.experimental.pallas import tpu as pltpu

D_MODEL = 1024
CTX_LEN = 256
GRID_W = 64
HEAD_DIM = 64
D_POOL = 256
D_RWKV = 384
D_HGRN = 384
POOL_WINDOWS = (2, 4, 8, 16)
POOL_GROUP = 64
LORA = 64
D_IN = 4224
LN_EPS = 1e-5
GN_EPS = 64e-5
RMS_EPS = 1e-6

TOK_BLOCK = 256
CHUNK = 64
PAIR = 2 * HEAD_DIM
N_PAIR = D_RWKV // PAIR
BATCH_BLOCK = 16
PREP_GROUP = 4
WAVE_SKEW = 1
CHUNK_LEVELS = (1, 2, 4, 8, 16, 32)
HALO = 8
VMEM_LIMIT_BYTES = 56 * 1024 * 1024

COL_GATE = 0
COL_POOL = 1024
COL_LORA = 1280
COL_HEADS = 1536

_F32 = jnp.float32
_HI = lax.Precision.HIGHEST


def _softplus(x):
    return jnp.maximum(x, 0.0) + jnp.log1p(jnp.exp(-jnp.abs(x)))


def _dot_exact(a, b):
    return jnp.dot(a, b, precision=_HI, preferred_element_type=_F32)


def _bf16(x):
    return x.astype(jnp.bfloat16)


def _dot(a, b):
    return jnp.dot(_bf16(a), _bf16(b), preferred_element_type=_F32)


def _dot_split(table2, x):
    head = _bf16(x)
    rest = _bf16(x - head.astype(_F32))
    return jnp.dot(_bf16(table2), jnp.concatenate([head, rest], axis=0), preferred_element_type=_F32)


def _dot_nt(a, b):
    return lax.dot_general(_bf16(a), _bf16(b), (((1,), (1,)), ((), ())), preferred_element_type=_F32)


def _dot_tn(a, b):
    return lax.dot_general(_bf16(a), _bf16(b), (((0,), (0,)), ((), ())), preferred_element_type=_F32)


def _chunk_pos(direction):
    t = np.arange(CHUNK)
    return t if direction == 0 else CHUNK - 1 - t


def _level_tables(direction):
    pos = _chunk_pos(direction)
    masks, sums = [], []
    for m in CHUNK_LEVELS:
        blk = pos // m
        pair = pos // (2 * m)
        late = (blk % 2) == 1
        same = pair[:, None] == pair[None, :]
        masks.append(same & late[:, None] & (~late)[None, :])
        late_rows = same & late[:, None] & late[None, :] & (pos[None, :] <= pos[:, None])
        early_rows = same & (~late)[:, None] & (~late)[None, :] & (pos[None, :] > pos[:, None])
        sums.append(late_rows | early_rows)
    return masks, sums


def _tile2(m):
    return np.tile(m.astype(np.float32), (1, 2))


@functools.lru_cache(maxsize=None)
def _scan_tables():
    rw_masks, hg_masks, cums, hg_sums = [], [], [], []
    for d in range(2):
        pos = _chunk_pos(d)
        strict = pos[None, :] < pos[:, None]
        incl = pos[None, :] <= pos[:, None]
        lv_masks, lv_sums = _level_tables(d)
        eye = np.eye(CHUNK, dtype=bool)
        rw_masks.append(np.stack([_tile2(strict), _tile2(incl)] + [_tile2(m) for m in lv_masks]
                                 + [-_tile2(strict)]))
        hg_masks.append(np.stack([_tile2(m) for m in lv_masks] + [_tile2(eye)]))
        cums.append(np.tile(incl.astype(np.float32), (1, 2)))
        hg_sums.append(np.tile(np.concatenate([incl] + lv_sums, axis=0).astype(np.float32), (1, 2)))
    return (np.stack(rw_masks), np.stack(hg_masks), np.stack(cums), np.stack(hg_sums))


@functools.lru_cache(maxsize=None)
def _pool_tables():
    out = np.zeros((2, len(POOL_WINDOWS), TOK_BLOCK, TOK_BLOCK), np.float32)
    for kind, seg in enumerate((TOK_BLOCK, GRID_W)):
        for g, win in enumerate(POOL_WINDOWS):
            left = win // 2
            right = win - 1 - left
            for t in range(TOK_BLOCK):
                base = (t // seg) * seg
                lo = max(t - left, base)
                hi = min(t + right, base + seg - 1) + 1
                out[kind, g, t, lo:hi] = 1.0 / (hi - lo)
                out[kind, g, t, t] -= 1.0
    return out


@functools.lru_cache(maxsize=None)
def _head_mean_table():
    h = np.arange(D_RWKV) // HEAD_DIM
    return (h[:, None] == h[None, :]).astype(np.float32) / HEAD_DIM


@functools.lru_cache(maxsize=None)
def _column_perm():
    pool = np.arange(0, 256)
    u0 = 256
    r = u0 + np.arange(0, 384)
    k = u0 + np.arange(384, 768)
    v = u0 + np.arange(768, 1152)
    lora = []
    for d in range(2):
        lora.append(u0 + 1152 + d * LORA + np.arange(LORA))
        lora.append(u0 + 1280 + d * LORA + np.arange(LORA))
    f0 = 256 + 1408
    q = f0 + np.arange(0, 384)
    i = f0 + np.arange(384, 768)
    z0 = f0 + np.arange(768, 1152)
    z1 = f0 + np.arange(1152, 1536)
    gate = np.arange(3200, 4224)
    perm = np.concatenate([gate, pool] + lora + [r, k, v, q, i, z0, z1])
    assert perm.shape[0] == D_IN and np.unique(perm).shape[0] == D_IN
    return perm


def _mod_kernel(c_ref, w_ref, b_ref, o_ref):
    c = c_ref[...]
    o_ref[0] = _dot_exact(c * jax.nn.sigmoid(c), w_ref[0]) + b_ref[0]


def _modulation(c_all, mod_w, mod_b):
    depth = mod_w.shape[0]
    rows = c_all.shape[0]
    n_tile = 1024
    return pl.pallas_call(
        _mod_kernel,
        grid=(depth, 3 * D_MODEL // n_tile),
        in_specs=[
            pl.BlockSpec((rows, D_MODEL), lambda l, n: (0, 0)),
            pl.BlockSpec((1, D_MODEL, n_tile), lambda l, n: (l, 0, n)),
            pl.BlockSpec((1, 1, n_tile), lambda l, n: (l, 0, n)),
        ],
        out_specs=pl.BlockSpec((1, rows, n_tile), lambda l, n: (l, 0, n)),
        out_shape=jax.ShapeDtypeStruct((depth, rows, 3 * D_MODEL), _F32),
        compiler_params=pltpu.CompilerParams(
            dimension_semantics=("arbitrary", "arbitrary"), vmem_limit_bytes=VMEM_LIMIT_BYTES),
        name="modulation",
    )(c_all, mod_w, mod_b.reshape(depth, 1, 3 * D_MODEL))


def _layer_norm(x):
    mu = jnp.mean(x, axis=-1, keepdims=True)
    xc = x - mu
    var = jnp.mean(xc * xc, axis=-1, keepdims=True)
    return xc * lax.rsqrt(var + LN_EPS)


def _stream_specs(stream, off=0):
    if len(stream) == 1:
        return [pl.BlockSpec((1, TOK_BLOCK, D_MODEL), lambda b, j: (b, j + off, 0))]
    return [pl.BlockSpec((1, TOK_BLOCK, D_MODEL), lambda b, j: (b, 0, 0)),
            pl.BlockSpec((1, TOK_BLOCK, D_MODEL), lambda b, j: (b, jnp.maximum(j + off - 1, 0), 0))]


def _stream_block(h_refs, off=0):
    if len(h_refs) == 1:
        return h_refs[0][0]
    ctx_ref, x_ref = h_refs
    return jnp.where(pl.program_id(1) + off == 0, ctx_ref[0], x_ref[0])


def _proj_kernel(*refs):
    *h_refs, sh_ref, sc_ref, w_ref, p_ref = refs
    xn = _layer_norm(_stream_block(h_refs)) * (1.0 + sc_ref[0]) + sh_ref[0]
    p_ref[0] = jnp.dot(xn.astype(jnp.bfloat16), w_ref[...], preferred_element_type=_F32)


def _mod_row(batch):
    return lambda b, j: (jnp.where(j == 0, batch, b), 0, 0)


def _input_projection(stream, shift, scale, w_in_bf16):
    batch = stream[0].shape[0]
    n_tok = sum(part.shape[1] for part in stream)
    return pl.pallas_call(
        _proj_kernel,
        grid=(batch, n_tok // TOK_BLOCK),
        in_specs=_stream_specs(stream) + [
            pl.BlockSpec((1, 1, D_MODEL), _mod_row(batch)),
            pl.BlockSpec((1, 1, D_MODEL), _mod_row(batch)),
            pl.BlockSpec((D_MODEL, D_IN), lambda b, j: (0, 0)),
        ],
        out_specs=pl.BlockSpec((1, TOK_BLOCK, D_IN), lambda b, j: (b, j, 0)),
        out_shape=jax.ShapeDtypeStruct((batch, n_tok, D_IN), _F32),
        compiler_params=pltpu.CompilerParams(
            dimension_semantics=("parallel", "parallel"), vmem_limit_bytes=VMEM_LIMIT_BYTES),
        name="input_projection",
    )(*stream, shift, scale, w_in_bf16)


def _scan_chunk(n_chunk):
    n_ctx = CTX_LEN // CHUNK
    return lambda d, j: jnp.where(d == 0, j, jnp.where(j < n_ctx, n_ctx - 1 - j, n_chunk + n_ctx - 1 - j))


def _batch_block(batch):
    return math.gcd(batch, BATCH_BLOCK)


def _block_diag_pair(x16, lo16, hi16):
    return jnp.concatenate([x16 * lo16, x16 * hi16], axis=0)


def _pair_masks():
    lane = lax.broadcasted_iota(jnp.int32, (CHUNK, PAIR), 1)
    lo16 = (lane < HEAD_DIM).astype(jnp.bfloat16)
    row = lax.broadcasted_iota(jnp.int32, (PAIR, PAIR), 0)
    col = lax.broadcasted_iota(jnp.int32, (PAIR, PAIR), 1)
    same_head = ((row < HEAD_DIM) == (col < HEAD_DIM)).astype(_F32)
    return lo16, 1.0 - lo16, same_head


def _run_wavefront(programs):
    live = [True] * len(programs)
    rnd = 0
    while any(live):
        for g in reversed(range(len(programs))):
            if live[g] and rnd >= WAVE_SKEW * g:
                try:
                    next(programs[g])
                except StopIteration:
                    live[g] = False
        rnd += 1


def _rwkv_programs(r_ref, k_ref, v_ref, lo_ref,
                   rp_ref, kp_ref, vp_ref, lp_ref, rn_ref, kn_ref, vn_ref, ln_ref,
                   shr_ref, shk_ref, shv_ref, shl_ref,
                   w0_ref, wup_ref, a0_ref, aup_ref, kk_ref, ka_ref, rk_ref,
                   hm_ref, mask_ref, cum_ref,
                   y_ref, bonus_ref, state_ref, *, n_chunk, n_bb):
    d = pl.program_id(1)
    j = pl.program_id(2)
    chunk = _scan_chunk(n_chunk)(d, j)
    n_ctx = CTX_LEN // CHUNK
    prev_ok = jnp.where((chunk == 0) | (chunk == n_ctx), 0.0, 1.0).astype(_F32)
    next_ok = jnp.where((chunk == n_ctx - 1) | (chunk == n_chunk - 1), 0.0, 1.0).astype(_F32)

    @pl.when(j == 0)
    def _():
        state_ref[...] = jnp.zeros_like(state_ref)

    def shifted(main_ref, prev_ref, next_ref, w_ref, bbs):
        w = w_ref[0]
        outs = []
        for bb in bbs:
            x = main_ref[bb]
            row = lax.broadcasted_iota(jnp.int32, x.shape, 0)
            prev_row = prev_ref[bb, HALO - 1:HALO, :] * prev_ok
            next_row = next_ref[bb, 0:1, :] * next_ok
            x_prev = jnp.where(row == 0, prev_row, pltpu.roll(x, 1, axis=0))
            x_next = jnp.where(row == CHUNK - 1, next_row, pltpu.roll(x, CHUNK - 1, axis=0))
            outs.append(w[0:1] * x_prev + w[1:2] * x + w[2:3] * x_next)
        return jnp.concatenate(outs, axis=0)

    group_size = math.gcd(n_bb, PREP_GROUP)
    groups = {}

    def group_prep(g):
        if g in groups:
            return groups[g]
        bbs = range(g * group_size, (g + 1) * group_size)
        r = shifted(r_ref, rp_ref, rn_ref, shr_ref, bbs)
        k = shifted(k_ref, kp_ref, kn_ref, shk_ref, bbs)
        v = shifted(v_ref, vp_ref, vn_ref, shv_ref, bbs)
        lora = shifted(lo_ref, lp_ref, ln_ref, shl_ref, bbs)
        w_pre = w0_ref[0] + _dot(jnp.tanh(lora), wup_ref[0])
        w_log = -_softplus(-w_pre) - 0.5
        lw = -jnp.exp(w_log)
        a = jax.nn.sigmoid(a0_ref[0] + _dot(lora, aup_ref[0]))
        kk = k * kk_ref[...]
        kk = kk * lax.rsqrt(_dot(kk * kk, hm_ref[...]) * HEAD_DIM + 1e-12)
        k_mod = k * (1.0 + (a - 1.0) * ka_ref[...])
        bonus = _dot(r * k_mod * rk_ref[0], hm_ref[...]) * HEAD_DIM * v
        for i, bb in enumerate(bbs):
            bonus_ref[0, bb] = bonus[i * CHUNK:(i + 1) * CHUNK]
        groups[g] = dict(r=r, v=v, lw=lw, kk=kk, k_mod=k_mod, beta=a * kk)
        return groups[g]

    lo16, hi16, same_head = _pair_masks()
    bd = lambda x16: _block_diag_pair(x16, lo16, hi16)
    row_c = lax.broadcasted_iota(jnp.int32, (CHUNK, PAIR), 0)
    lane_c = lax.broadcasted_iota(jnp.int32, (CHUNK, PAIR), 1)
    eye_ls = ((lane_c == row_c) | (lane_c == row_c + HEAD_DIM)).astype(_F32)
    eye = (lax.broadcasted_iota(jnp.int32, (PAIR, PAIR), 0)
           == lax.broadcasted_iota(jnp.int32, (PAIR, PAIR), 1)).astype(_F32)
    zeros16 = jnp.zeros((PAIR, PAIR), jnp.bfloat16)
    zeros_c16 = jnp.zeros((CHUNK, PAIR), jnp.bfloat16)
    m_strict = mask_ref[0, 0]
    m_incl2 = jnp.concatenate([mask_ref[0, 1], mask_ref[0, 1]], axis=1)
    m_neg_strict = mask_ref[0, 2 + len(CHUNK_LEVELS)]
    lv_masks = [mask_ref[0, 2 + lv] for lv in range(len(CHUNK_LEVELS))]
    cum = cum_ref[0]
    lanes = lambda p: slice(p * PAIR, (p + 1) * PAIR)
    pairs = range(N_PAIR)
    results = []

    def program(bb):
        tok = group_prep(bb // group_size)
        rows = slice((bb % group_size) * CHUNK, (bb % group_size + 1) * CHUNK)
        lw_c = tok["lw"][rows]
        g_incl = _dot_split(cum, lw_c)
        yield
        g_excl = g_incl - lw_c
        g_mid = g_incl[CHUNK // 2:CHUNK // 2 + 1, :]
        g_end = jnp.sum(lw_c, axis=0, keepdims=True)
        e_in = jnp.exp(g_incl - g_mid)
        e_out = jnp.exp(g_mid - g_incl)
        e_end = jnp.exp(g_end - g_incl)
        p_end = jnp.exp(g_end)
        p_mid = jnp.exp(g_mid)
        kkd = tok["kk"][rows] * jnp.exp(g_excl - g_mid)
        rd = tok["r"][rows] * e_in
        k_c, b_c = tok["k_mod"][rows], tok["beta"][rows]
        ch = {name: _bf16(val) for name, val in dict(
            kkd=kkd, kkd_n=kkd * (-p_mid), rd=rd, ki=k_c * e_out, bi=b_c * e_out,
            kd=k_c * e_end, bd=b_c * e_end, v=tok["v"][rows]).items()}
        nat = lambda p, name: ch[name][:, lanes(p)]
        rd_abs = rd * p_mid

        gm = [_dot_nt(jnp.concatenate([nat(p, "kkd"), nat(p, "rd")], axis=0),
                      jnp.concatenate([bd(nat(p, "ki")), bd(nat(p, "bi"))], axis=0))
              for p in pairs]
        yield
        g16 = [_bf16(g) for g in gm]
        a_kn = [g[:CHUNK, :PAIR] * m_neg_strict for g in g16]
        a_b = [g[:CHUNK, PAIR:] * m_strict for g in g16]
        r_kb = [g[CHUNK:, :] * m_incl2 for g in g16]
        v_bd = [bd(nat(p, "v")) for p in pairs]
        t_inv = [eye_ls - (a_b[p] * lv_masks[0]).astype(_F32) for p in pairs]
        w0n = [_dot(a_kn[p], v_bd[p]) for p in pairs]
        for lv in range(1, len(CHUNK_LEVELS)):
            t16 = [_bf16(t) for t in t_inv]
            x = [_dot(t16[p], bd(a_b[p] * lv_masks[lv])) for p in pairs]
            yield
            t_new = [_dot(x[p], bd(t16[p])) for p in pairs]
            yield
            t_inv = [t_inv[p] - t_new[p] for p in pairs]
        twk = [_dot(t_inv[p], jnp.concatenate([bd(_bf16(w0n[p])), bd(nat(p, "kkd_n"))], axis=1)) for p in pairs]
        yield
        t16 = [_bf16(t) for t in twk]
        z = [jnp.concatenate([jnp.concatenate([v_bd[p], zeros16], axis=1),
                              jnp.concatenate([bd(t16[p][:, :PAIR]), bd(t16[p][:, PAIR:])], axis=1)], axis=0)
             for p in pairs]
        yr = [_dot(r_kb[p], z[p]) for p in pairs]
        mn = [_dot_tn(jnp.concatenate([nat(p, "kd"), nat(p, "bd")], axis=0),
                      jnp.concatenate([jnp.concatenate([nat(p, "v"), zeros_c16], axis=1), t16[p]], axis=0))
              for p in pairs]
        yield
        st = [state_ref[bb * N_PAIR + p] for p in pairs]
        y_new = [yr[p][:, :PAIR] + _dot(rd_abs[:, lanes(p)] + yr[p][:, PAIR:], st[p]) for p in pairs]
        st_new = [_dot(mn[p][:, PAIR:] * same_head + eye * p_end[:, lanes(p)], st[p])
                  + mn[p][:, :PAIR] * same_head for p in pairs]
        results.append((bb, y_new, st_new))

    def finish():
        for bb, y_new, st_new in results:
            for p in pairs:
                y_ref[0, bb, :, lanes(p)] = y_new[p]
                state_ref[bb * N_PAIR + p] = st_new[p]

    return [program(bb) for bb in range(n_bb)], finish


def _hgrn_programs(q_ref, i_ref, z_ref, lb_ref, mask_ref, sum_ref, o_ref, state_ref, *, layer, n_bb):
    j = pl.program_id(2)

    @pl.when(j == 0)
    def _():
        state_ref[...] = jnp.zeros_like(state_ref)

    logits = lb_ref[0]
    e = jnp.exp(logits - jnp.max(logits, axis=0, keepdims=True))
    lb_w = e / jnp.sum(e, axis=0, keepdims=True)
    csum = lb_w[0:1]
    for l in range(1, layer + 1):
        csum = csum + lb_w[l:l + 1]
    lb = jnp.maximum(csum - lb_w[0:1], 0.0)
    log_lb = jnp.log(lb)
    log_1m_lb = jnp.log1p(-lb)

    lo16, hi16, same_head = _pair_masks()
    bd = lambda x16: _block_diag_pair(x16, lo16, hi16)
    n_lv = len(CHUNK_LEVELS)
    lv_mask = [mask_ref[0, n_lv]] + [mask_ref[0, lv] for lv in range(n_lv)]
    lanes = lambda p: slice(p * PAIR, (p + 1) * PAIR)
    sum_table = sum_ref[0]
    pairs = range(N_PAIR)
    results = []

    def program(bb):
        z = z_ref[bb]
        log_rest = log_1m_lb - _softplus(-z)
        logf = jnp.maximum(log_lb, log_rest) + jnp.log1p(jnp.exp(-jnp.abs(log_lb - log_rest)))
        k_c = (1.0 - lb) * jax.nn.sigmoid(-z)
        q_c = q_ref[bb]
        i_c = i_ref[bb]
        sums = _dot_split(sum_table, logf)
        yield
        g_incl = sums[:CHUNK]
        g_end = jnp.min(g_incl, axis=0, keepdims=True)
        p_end = jnp.exp(g_end)
        qd = q_c * jnp.exp(g_incl)
        kd = k_c * jnp.exp(g_end - g_incl)
        st = [state_ref[bb * N_PAIR + p] for p in pairs]
        inter = [_dot_nt(qd[:, lanes(p)], st[p]) for p in pairs]
        outer = [_dot_tn(i_c[:, lanes(p)], kd[:, lanes(p)]) for p in pairs]
        lv_q, lv_k = [_bf16(q_c)], [_bf16(k_c)]
        for lv in range(n_lv):
            x = jnp.exp(sums[(1 + lv) * CHUNK:(2 + lv) * CHUNK])
            lv_q.append(_bf16(q_c * x))
            lv_k.append(_bf16(k_c * x))
        i16 = _bf16(i_c)
        scores = []
        for p in pairs:
            parts = [_dot_nt(q_l[:, lanes(p)], bd(k_l[:, lanes(p)])) for q_l, k_l in zip(lv_q, lv_k)]
            yield
            sc = parts[0] * lv_mask[0]
            for part, m_l in zip(parts[1:], lv_mask[1:]):
                sc = sc + part * m_l
            scores.append(sc)
        intra = [_dot(scores[p], bd(i16[:, lanes(p)])) for p in pairs]
        st_new = [st[p] * p_end[:, lanes(p)] + outer[p] * same_head for p in pairs]
        yield
        results.append((bb, [intra[p] + inter[p] for p in pairs], st_new))

    def finish():
        for bb, o_new, st_new in results:
            for p in pairs:
                o_ref[0, bb, :, lanes(p)] = o_new[p]
                state_ref[bb * N_PAIR + p] = st_new[p]

    return [program(bb) for bb in range(n_bb)], finish


N_RWKV_IN = 26
N_HGRN_IN = 6


def _scan_kernel(*refs, n_chunk, n_bb, layer):
    rw_in = refs[:N_RWKV_IN]
    hg_in = refs[N_RWKV_IN:N_RWKV_IN + N_HGRN_IN]
    y_ref, bonus_ref, o_ref, rw_state, hg_state = refs[N_RWKV_IN + N_HGRN_IN:]
    rw_programs, rw_finish = _rwkv_programs(*rw_in, y_ref, bonus_ref, rw_state, n_chunk=n_chunk, n_bb=n_bb)
    hg_programs, hg_finish = _hgrn_programs(*hg_in, o_ref, hg_state, layer=layer, n_bb=n_bb)
    programs = [prog for pair in zip(rw_programs, hg_programs) for prog in pair]
    _run_wavefront(programs)
    rw_finish()
    hg_finish()


def _mixer_scan(p_all, params, lb_logits, tables, layer):
    batch, n_tok, _ = p_all.shape
    n_bb = _batch_block(batch)
    n_chunk = n_tok // CHUNK
    n_halo = CHUNK // HALO
    head0 = COL_HEADS // D_RWKV
    lora0 = COL_LORA // PAIR
    chunk_of = _scan_chunk(n_chunk)

    def main(col_of, width=D_RWKV):
        return pl.BlockSpec((n_bb, CHUNK, width), lambda b, d, j: (b, chunk_of(d, j), col_of(d)))

    def prev(col_of, width=D_RWKV):
        return pl.BlockSpec(
            (n_bb, HALO, width),
            lambda b, d, j: (b, jnp.maximum(chunk_of(d, j) * n_halo - 1, 0), col_of(d)))

    def nxt(col_of, width=D_RWKV):
        return pl.BlockSpec(
            (n_bb, HALO, width),
            lambda b, d, j: (b, jnp.minimum((chunk_of(d, j) + 1) * n_halo, n_tok // HALO - 1), col_of(d)))

    col = lambda c: (lambda d: c)
    cols = [col(head0), col(head0 + 1), col(head0 + 2)]
    lora_col = lambda d: lora0 + d
    full = lambda shape: pl.BlockSpec(shape, lambda b, d, j: (0,) * len(shape))
    per_dir = lambda shape: pl.BlockSpec((1,) + shape, lambda b, d, j: (d,) + (0,) * len(shape))
    out_spec = pl.BlockSpec((1, n_bb, CHUNK, D_RWKV), lambda b, d, j: (d, b, chunk_of(d, j), 0))

    rw_masks, hg_masks, cums, hg_sums = tables
    rw_masks = rw_masks.astype(jnp.bfloat16)
    rwkv_specs = (
        [main(c) for c in cols] + [main(lora_col, PAIR)]
        + [prev(c) for c in cols] + [prev(lora_col, PAIR)]
        + [nxt(c) for c in cols] + [nxt(lora_col, PAIR)]
        + [
            full((1, 3, D_RWKV)), full((1, 3, D_RWKV)), full((1, 3, D_RWKV)), per_dir((3, PAIR)),
            per_dir((1, D_RWKV)), per_dir((PAIR, D_RWKV)), per_dir((1, D_RWKV)), per_dir((PAIR, D_RWKV)),
            full((1, D_RWKV)), full((1, D_RWKV)), per_dir((1, D_RWKV)),
            full((D_RWKV, D_RWKV)), per_dir(rw_masks.shape[1:]), per_dir(cums.shape[1:]),
        ])
    hgrn_specs = [
        main(col(head0 + 3)), main(col(head0 + 4)), main(lambda d: head0 + 5 + d),
        per_dir(lb_logits.shape[1:]), per_dir(hg_masks.shape[1:]), per_dir(hg_sums.shape[1:]),
    ]
    assert len(rwkv_specs) == N_RWKV_IN and len(hgrn_specs) == N_HGRN_IN
    out_struct = jax.ShapeDtypeStruct((2, batch, n_tok, D_RWKV), _F32)
    return pl.pallas_call(
        functools.partial(_scan_kernel, n_chunk=n_chunk, n_bb=n_bb, layer=layer),
        grid=(batch // n_bb, 2, n_chunk),
        in_specs=rwkv_specs + hgrn_specs,
        out_specs=[out_spec, out_spec, out_spec],
        out_shape=[out_struct, out_struct, out_struct],
        scratch_shapes=[pltpu.VMEM((n_bb * N_PAIR, PAIR, PAIR), _F32),
                        pltpu.VMEM((n_bb * N_PAIR, PAIR, PAIR), _F32)],
        compiler_params=pltpu.CompilerParams(
            dimension_semantics=("parallel", "arbitrary", "arbitrary"),
            vmem_limit_bytes=VMEM_LIMIT_BYTES),
        name="mixer_scan",
    )(p_all, p_all, p_all, p_all, p_all, p_all, p_all, p_all, p_all, p_all, p_all, p_all,
      params["shift_r"], params["shift_k"], params["shift_v"], params["shift_lora"],
      params["w0"], params["w_up"], params["a0"], params["a_up"],
      params["k_k"], params["k_a"], params["r_k"],
      _head_mean_table(), rw_masks, cums,
      p_all, p_all, p_all, lb_logits, hg_masks, hg_sums)


def _readout_kernel(*refs, alpha, off):
    (*h_refs, gate_ref, pool_ref, yf_ref, yb_ref, bf_ref, bb_ref, of_ref, ob_ref,
     gt_ref, pm_ref, pw_ref, ps_ref, gng_ref, gnb_ref, ng_ref, hm_ref,
     wout_ref, lng_ref, lnb_ref, out_ref) = refs
    pv = pool_ref[0]
    group = lax.broadcasted_iota(jnp.int32, pv.shape, 1) // POOL_GROUP
    pooled = jnp.zeros_like(pv)
    for g in range(len(POOL_WINDOWS)):
        pooled = pooled + jnp.where(group == g, _dot(pm_ref[0, g], pv), 0.0)
    mixed = _dot(pooled, pw_ref[...]) * ps_ref[...]

    y = yf_ref[0, 0] + yb_ref[0, 0]
    yc = y - _dot(y, hm_ref[...])
    var = _dot(yc * yc, hm_ref[...])
    rw = yc * lax.rsqrt(var + GN_EPS) * gng_ref[...] + gnb_ref[...] + bf_ref[0, 0] + bb_ref[0, 0]

    o = of_ref[0, 0] + ob_ref[0, 0]
    ho = o * lax.rsqrt(_dot(o * o, hm_ref[...]) + RMS_EPS) * ng_ref[...]

    gate = gate_ref[0]
    mix = jnp.concatenate([mixed, rw, ho], axis=1) * (gate * jax.nn.sigmoid(gate))
    proj = jnp.dot(mix.astype(jnp.bfloat16), wout_ref[...], preferred_element_type=_F32)
    h = _stream_block(h_refs, off)
    out_ref[0] = _layer_norm(alpha * h + gt_ref[0] * proj) * lng_ref[...] + lnb_ref[...]


def _readout(stream, p_all, y, bonus, o, gate_mod, params, alpha, skip_ctx):
    batch, n_tok, _ = p_all.shape
    n_blk = n_tok // TOK_BLOCK
    off = 1 if skip_ctx else 0
    tok = lambda width, col: pl.BlockSpec((1, TOK_BLOCK, width), lambda b, j: (b, j + off, col))
    dirs = lambda d: pl.BlockSpec((1, 1, TOK_BLOCK, D_RWKV), lambda b, j: (d, b, j + off, 0))
    full = lambda shape: pl.BlockSpec(shape, lambda b, j: (0,) * len(shape))
    n_win = len(POOL_WINDOWS)
    return pl.pallas_call(
        functools.partial(_readout_kernel, alpha=alpha, off=off),
        grid=(batch, n_blk - off),
        in_specs=_stream_specs(stream, off) + [
            tok(D_MODEL, COL_GATE // D_MODEL), tok(D_POOL, COL_POOL // D_POOL),
            dirs(0), dirs(1), dirs(0), dirs(1), dirs(0), dirs(1),
            pl.BlockSpec((1, 1, D_MODEL), lambda b, j: (jnp.where(j + off == 0, batch, b), 0, 0)),
            pl.BlockSpec((1, n_win, TOK_BLOCK, TOK_BLOCK),
                         lambda b, j: (jnp.where(j + off == 0, 0, 1), 0, 0, 0)),
            full((D_POOL, D_POOL)), full((1, D_POOL)),
            full((1, D_RWKV)), full((1, D_RWKV)), full((1, D_HGRN)), full((D_RWKV, D_RWKV)),
            full((D_MODEL, D_MODEL)), full((1, D_MODEL)), full((1, D_MODEL)),
        ],
        out_specs=pl.BlockSpec((1, TOK_BLOCK, D_MODEL), lambda b, j: (b, j, 0)),
        out_shape=jax.ShapeDtypeStruct((batch, n_tok - off * TOK_BLOCK, D_MODEL), _F32),
        compiler_params=pltpu.CompilerParams(
            dimension_semantics=("parallel", "parallel"), vmem_limit_bytes=VMEM_LIMIT_BYTES),
        name="readout",
    )(*stream, p_all, p_all, y, y, bonus, bonus, o, o, gate_mod,
      jnp.asarray(_pool_tables()), params["pool_w"], params["pool_scale"],
      params["gn_g"], params["gn_b"], params["norm_g"], _head_mean_table(),
      params["w_out"], params["ln_g"], params["ln_b"])


def _block_diag(blocks):
    n, a, b = blocks.shape
    out = jnp.zeros((n * a, n * b), blocks.dtype)
    for g in range(n):
        out = out.at[g * a:(g + 1) * a, g * b:(g + 1) * b].set(blocks[g])
    return out


def _permute_columns(w):
    perm = _column_perm()
    cuts = [0] + [i for i in range(1, len(perm)) if perm[i] != perm[i - 1] + 1] + [len(perm)]
    return jnp.concatenate([w[:, int(perm[a]):int(perm[b - 1]) + 1] for a, b in zip(cuts[:-1], cuts[1:])], axis=1)


def _layer_params(l, w_in, rwkv_shift, pool_w, pool_scale, rwkv_w0, rwkv_w_up, rwkv_a0, rwkv_a_up,
                  rwkv_k_k, rwkv_k_a, rwkv_r_k, rwkv_gn_g, rwkv_gn_b, hgrn_norm_g, w_out, ln_g, ln_b):
    sh = rwkv_shift[l]
    lora_sh = jnp.stack([
        jnp.concatenate([sh[:, 1152 + d * LORA:1152 + (d + 1) * LORA],
                         sh[:, 1280 + d * LORA:1280 + (d + 1) * LORA]], axis=1) for d in range(2)])
    pad = jnp.zeros((2, LORA, D_RWKV), _F32)
    return {
        "w_in": _permute_columns(w_in[l]).astype(jnp.bfloat16),
        "shift_r": sh[None, :, 0:384], "shift_k": sh[None, :, 384:768], "shift_v": sh[None, :, 768:1152],
        "shift_lora": lora_sh,
        "w0": rwkv_w0[l][:, None, :], "a0": rwkv_a0[l][:, None, :],
        "w_up": jnp.concatenate([rwkv_w_up[l], pad], axis=1),
        "a_up": jnp.concatenate([pad, rwkv_a_up[l]], axis=1),
        "k_k": rwkv_k_k[l][None], "k_a": rwkv_k_a[l][None],
        "r_k": rwkv_r_k[l].reshape(2, 1, D_RWKV),
        "pool_w": _block_diag(pool_w[l]), "pool_scale": pool_scale[l][None],
        "gn_g": rwkv_gn_g[l][None], "gn_b": rwkv_gn_b[l][None], "norm_g": hgrn_norm_g[l][None],
        "w_out": w_out[l].astype(jnp.bfloat16), "ln_g": ln_g[l][None], "ln_b": ln_b[l][None],
    }


def kernel(x, c, ctx, c_ctx, mod_w, mod_b, w_in, rwkv_shift, pool_w, pool_scale, rwkv_w0, rwkv_w_up,
           rwkv_a0, rwkv_a_up, rwkv_k_k, rwkv_k_a, rwkv_r_k, rwkv_gn_g, rwkv_gn_b, hgrn_lb_logits,
           hgrn_norm_g, w_out, ln_g, ln_b):
    depth = mod_w.shape[0]
    batch = x.shape[0]
    assert ctx.shape[1] == CTX_LEN == TOK_BLOCK and x.shape[1] % TOK_BLOCK == 0
    alpha = float((2 * depth) ** 0.25)
    tables = tuple(jnp.asarray(t) for t in _scan_tables())

    c_all = jnp.concatenate([c.astype(_F32), c_ctx.astype(_F32)[None]], axis=0)
    mod = _modulation(c_all, mod_w, mod_b)
    stream = (ctx.astype(_F32), x.astype(_F32))

    for l in range(depth):
        prm = _layer_params(l, w_in, rwkv_shift, pool_w, pool_scale, rwkv_w0, rwkv_w_up, rwkv_a0,
                            rwkv_a_up, rwkv_k_k, rwkv_k_a, rwkv_r_k, rwkv_gn_g, rwkv_gn_b,
                            hgrn_norm_g, w_out, ln_g, ln_b)
        shift = mod[l, :, None, 0:D_MODEL]
        scale = mod[l, :, None, D_MODEL:2 * D_MODEL]
        gate_mod = mod[l, :, None, 2 * D_MODEL:]
        p_all = _input_projection(stream, shift, scale, prm["w_in"])
        y, bonus, o = _mixer_scan(p_all, prm, hgrn_lb_logits, tables, l)
        stream = (_readout(stream, p_all, y, bonus, o, gate_mod, prm, alpha, skip_ctx=(l == depth - 1)),)
    return stream[0].astype(x.dtype)
```

```python
import functools
import math

import numpy as np
import jax
import jax.numpy as jnp
from jax import lax
from jax.experimental import pallas as pl
from jax.experimental.pallas import tpu as pltpu

D_MODEL = 1024
CTX_LEN = 256
GRID_W = 64
HEAD_DIM = 64
D_POOL = 256
D_RWKV = 384
D_HGRN = 384
POOL_WINDOWS = (2, 4, 8, 16)
POOL_GROUP = 64
LORA = 64
D_IN = 4224
LN_EPS = 1e-5
GN_EPS = 64e-5
RMS_EPS = 1e-6

TOK_BLOCK = 256
CHUNK = 64
PAIR = 2 * HEAD_DIM
N_PAIR = D_RWKV // PAIR
BATCH_BLOCK = 16
READOUT_ROWS = 2
PREP_GROUP = 4
WAVE_SKEW = 1
CHUNK_LEVELS = (1, 2, 4, 8, 16, 32)
HALO = 8
VMEM_LIMIT_BYTES = 56 * 1024 * 1024

COL_GATE = 0
COL_POOL = 1024
COL_LORA = 1280
COL_HEADS = 1536

_F32 = jnp.float32
_HI = lax.Precision.HIGHEST


def _softplus(x):
    return jnp.maximum(x, 0.0) + jnp.log1p(jnp.exp(-jnp.abs(x)))


def _dot_exact(a, b):
    return jnp.dot(a, b, precision=_HI, preferred_element_type=_F32)


def _bf16(x):
    return x.astype(jnp.bfloat16)


def _dot(a, b):
    return jnp.dot(_bf16(a), _bf16(b), preferred_element_type=_F32)


def _dot_split(table2, x):
    head = _bf16(x)
    rest = _bf16(x - head.astype(_F32))
    return jnp.dot(_bf16(table2), jnp.concatenate([head, rest], axis=0), preferred_element_type=_F32)


def _dot_nt(a, b):
    return lax.dot_general(_bf16(a), _bf16(b), (((1,), (1,)), ((), ())), preferred_element_type=_F32)


def _dot_tn(a, b):
    return lax.dot_general(_bf16(a), _bf16(b), (((0,), (0,)), ((), ())), preferred_element_type=_F32)


def _chunk_pos(direction):
    t = np.arange(CHUNK)
    return t if direction == 0 else CHUNK - 1 - t


def _level_tables(direction):
    pos = _chunk_pos(direction)
    masks, sums = [], []
    for m in CHUNK_LEVELS:
        blk = pos // m
        pair = pos // (2 * m)
        late = (blk % 2) == 1
        same = pair[:, None] == pair[None, :]
        masks.append(same & late[:, None] & (~late)[None, :])
        late_rows = same & late[:, None] & late[None, :] & (pos[None, :] <= pos[:, None])
        early_rows = same & (~late)[:, None] & (~late)[None, :] & (pos[None, :] > pos[:, None])
        sums.append(late_rows | early_rows)
    return masks, sums


def _tile2(m):
    return np.tile(m.astype(np.float32), (1, 2))


@functools.lru_cache(maxsize=None)
def _scan_tables():
    rw_masks, hg_masks, cums, hg_sums = [], [], [], []
    for d in range(2):
        pos = _chunk_pos(d)
        strict = pos[None, :] < pos[:, None]
        incl = pos[None, :] <= pos[:, None]
        lv_masks, lv_sums = _level_tables(d)
        eye = np.eye(CHUNK, dtype=bool)
        rw_masks.append(np.stack([_tile2(strict), _tile2(incl)] + [_tile2(m) for m in lv_masks]
                                 + [-_tile2(strict)]))
        hg_masks.append(np.stack([_tile2(m) for m in lv_masks] + [_tile2(eye)]))
        cums.append(np.tile(incl.astype(np.float32), (1, 2)))
        hg_sums.append(np.tile(np.concatenate([incl] + lv_sums, axis=0).astype(np.float32), (1, 2)))
    return (np.stack(rw_masks), np.stack(hg_masks), np.stack(cums), np.stack(hg_sums))


@functools.lru_cache(maxsize=None)
def _pool_tables():
    out = np.zeros((2, len(POOL_WINDOWS), TOK_BLOCK, TOK_BLOCK), np.float32)
    for kind, seg in enumerate((TOK_BLOCK, GRID_W)):
        for g, win in enumerate(POOL_WINDOWS):
            left = win // 2
            right = win - 1 - left
            for t in range(TOK_BLOCK):
                base = (t // seg) * seg
                lo = max(t - left, base)
                hi = min(t + right, base + seg - 1) + 1
                out[kind, g, t, lo:hi] = 1.0 / (hi - lo)
                out[kind, g, t, t] -= 1.0
    return out


@functools.lru_cache(maxsize=None)
def _head_mean_table():
    h = np.arange(D_RWKV) // HEAD_DIM
    return (h[:, None] == h[None, :]).astype(np.float32) / HEAD_DIM


@functools.lru_cache(maxsize=None)
def _column_perm():
    pool = np.arange(0, 256)
    u0 = 256
    r = u0 + np.arange(0, 384)
    k = u0 + np.arange(384, 768)
    v = u0 + np.arange(768, 1152)
    lora = []
    for d in range(2):
        lora.append(u0 + 1152 + d * LORA + np.arange(LORA))
        lora.append(u0 + 1280 + d * LORA + np.arange(LORA))
    f0 = 256 + 1408
    q = f0 + np.arange(0, 384)
    i = f0 + np.arange(384, 768)
    z0 = f0 + np.arange(768, 1152)
    z1 = f0 + np.arange(1152, 1536)
    gate = np.arange(3200, 4224)
    perm = np.concatenate([gate, pool] + lora + [r, k, v, q, i, z0, z1])
    assert perm.shape[0] == D_IN and np.unique(perm).shape[0] == D_IN
    return perm


def _mod_kernel(c_ref, w_ref, b_ref, o_ref):
    c = c_ref[...]
    o_ref[0] = _dot_exact(c * jax.nn.sigmoid(c), w_ref[0]) + b_ref[0]


def _modulation(c_all, mod_w, mod_b):
    depth = mod_w.shape[0]
    rows = c_all.shape[0]
    n_tile = 1024
    return pl.pallas_call(
        _mod_kernel,
        grid=(depth, 3 * D_MODEL // n_tile),
        in_specs=[
            pl.BlockSpec((rows, D_MODEL), lambda l, n: (0, 0)),
            pl.BlockSpec((1, D_MODEL, n_tile), lambda l, n: (l, 0, n)),
            pl.BlockSpec((1, 1, n_tile), lambda l, n: (l, 0, n)),
        ],
        out_specs=pl.BlockSpec((1, rows, n_tile), lambda l, n: (l, 0, n)),
        out_shape=jax.ShapeDtypeStruct((depth, rows, 3 * D_MODEL), _F32),
        compiler_params=pltpu.CompilerParams(
            dimension_semantics=("arbitrary", "arbitrary"), vmem_limit_bytes=VMEM_LIMIT_BYTES),
        name="modulation",
    )(c_all, mod_w, mod_b.reshape(depth, 1, 3 * D_MODEL))


def _layer_norm(x):
    mu = jnp.mean(x, axis=-1, keepdims=True)
    xc = x - mu
    var = jnp.mean(xc * xc, axis=-1, keepdims=True)
    return xc * lax.rsqrt(var + LN_EPS)


def _stream_specs(stream, off=0, rows=1):
    if len(stream) == 1:
        return [pl.BlockSpec((rows, TOK_BLOCK, D_MODEL), lambda b, j: (b, j + off, 0))]
    return [pl.BlockSpec((rows, TOK_BLOCK, D_MODEL), lambda b, j: (b, 0, 0)),
            pl.BlockSpec((rows, TOK_BLOCK, D_MODEL), lambda b, j: (b, jnp.maximum(j + off - 1, 0), 0))]


def _stream_block(h_refs, off=0, row=0):
    if len(h_refs) == 1:
        return h_refs[0][row]
    ctx_ref, x_ref = h_refs
    return jnp.where(pl.program_id(1) + off == 0, ctx_ref[row], x_ref[row])


def _proj_kernel(*refs):
    *h_refs, sh_ref, sc_ref, w_ref, p_ref = refs
    xn = _layer_norm(_stream_block(h_refs)) * (1.0 + sc_ref[0]) + sh_ref[0]
    p_ref[0] = jnp.dot(xn.astype(jnp.bfloat16), w_ref[...], preferred_element_type=_F32)


def _mod_row(batch):
    return lambda b, j: (jnp.where(j == 0, batch, b), 0, 0)


def _input_projection(stream, shift, scale, w_in_bf16):
    batch = stream[0].shape[0]
    n_tok = sum(part.shape[1] for part in stream)
    return pl.pallas_call(
        _proj_kernel,
        grid=(batch, n_tok // TOK_BLOCK),
        in_specs=_stream_specs(stream) + [
            pl.BlockSpec((1, 1, D_MODEL), _mod_row(batch)),
            pl.BlockSpec((1, 1, D_MODEL), _mod_row(batch)),
            pl.BlockSpec((D_MODEL, D_IN), lambda b, j: (0, 0)),
        ],
        out_specs=pl.BlockSpec((1, TOK_BLOCK, D_IN), lambda b, j: (b, j, 0)),
        out_shape=jax.ShapeDtypeStruct((batch, n_tok, D_IN), _F32),
        compiler_params=pltpu.CompilerParams(
            dimension_semantics=("parallel", "parallel"), vmem_limit_bytes=VMEM_LIMIT_BYTES),
        name="input_projection",
    )(*stream, shift, scale, w_in_bf16)


def _scan_chunk(n_chunk):
    n_ctx = CTX_LEN // CHUNK
    return lambda d, j: jnp.where(d == 0, j, jnp.where(j < n_ctx, n_ctx - 1 - j, n_chunk + n_ctx - 1 - j))


def _batch_block(batch):
    return math.gcd(batch, BATCH_BLOCK)


def _block_diag_pair(x16, lo16, hi16):
    return jnp.concatenate([x16 * lo16, x16 * hi16], axis=0)


def _pair_masks():
    lane = lax.broadcasted_iota(jnp.int32, (CHUNK, PAIR), 1)
    lo16 = (lane < HEAD_DIM).astype(jnp.bfloat16)
    row = lax.broadcasted_iota(jnp.int32, (PAIR, PAIR), 0)
    col = lax.broadcasted_iota(jnp.int32, (PAIR, PAIR), 1)
    same_head = ((row < HEAD_DIM) == (col < HEAD_DIM)).astype(_F32)
    return lo16, 1.0 - lo16, same_head


def _run_wavefront(programs):
    live = [True] * len(programs)
    rnd = 0
    while any(live):
        for g in reversed(range(len(programs))):
            if live[g] and rnd >= WAVE_SKEW * g:
                try:
                    next(programs[g])
                except StopIteration:
                    live[g] = False
        rnd += 1


def _rwkv_programs(r_ref, k_ref, v_ref, lo_ref,
                   rp_ref, kp_ref, vp_ref, lp_ref, rn_ref, kn_ref, vn_ref, ln_ref,
                   shr_ref, shk_ref, shv_ref, shl_ref,
                   w0_ref, wup_ref, a0_ref, aup_ref, kk_ref, ka_ref, rk_ref,
                   hm_ref, mask_ref, cum_ref,
                   y_ref, bonus_ref, state_ref, *, n_chunk, n_bb):
    d = pl.program_id(1)
    j = pl.program_id(2)
    chunk = _scan_chunk(n_chunk)(d, j)
    n_ctx = CTX_LEN // CHUNK
    prev_ok = jnp.where((chunk == 0) | (chunk == n_ctx), 0.0, 1.0).astype(_F32)
    next_ok = jnp.where((chunk == n_ctx - 1) | (chunk == n_chunk - 1), 0.0, 1.0).astype(_F32)

    @pl.when(j == 0)
    def _():
        state_ref[...] = jnp.zeros_like(state_ref)

    def shifted(main_ref, prev_ref, next_ref, w_ref, bbs):
        w = w_ref[0]
        outs = []
        for bb in bbs:
            x = main_ref[bb]
            row = lax.broadcasted_iota(jnp.int32, x.shape, 0)
            prev_row = prev_ref[bb, HALO - 1:HALO, :] * prev_ok
            next_row = next_ref[bb, 0:1, :] * next_ok
            x_prev = jnp.where(row == 0, prev_row, pltpu.roll(x, 1, axis=0))
            x_next = jnp.where(row == CHUNK - 1, next_row, pltpu.roll(x, CHUNK - 1, axis=0))
            outs.append(w[0:1] * x_prev + w[1:2] * x + w[2:3] * x_next)
        return jnp.concatenate(outs, axis=0)

    group_size = math.gcd(n_bb, PREP_GROUP)
    groups = {}

    def group_prep(g):
        if g in groups:
            return groups[g]
        bbs = range(g * group_size, (g + 1) * group_size)
        r = shifted(r_ref, rp_ref, rn_ref, shr_ref, bbs)
        k = shifted(k_ref, kp_ref, kn_ref, shk_ref, bbs)
        v = shifted(v_ref, vp_ref, vn_ref, shv_ref, bbs)
        lora = shifted(lo_ref, lp_ref, ln_ref, shl_ref, bbs)
        w_pre = w0_ref[0] + _dot(jnp.tanh(lora), wup_ref[0])
        w_log = -_softplus(-w_pre) - 0.5
        lw = -jnp.exp(w_log)
        a = jax.nn.sigmoid(a0_ref[0] + _dot(lora, aup_ref[0]))
        kk = k * kk_ref[...]
        kk = kk * lax.rsqrt(_dot(kk * kk, hm_ref[...]) * HEAD_DIM + 1e-12)
        k_mod = k * (1.0 + (a - 1.0) * ka_ref[...])
        bonus = _dot(r * k_mod * rk_ref[0], hm_ref[...]) * HEAD_DIM * v
        for i, bb in enumerate(bbs):
            bonus_ref[0, bb] = bonus[i * CHUNK:(i + 1) * CHUNK]
        groups[g] = dict(r=r, v=v, lw=lw, kk=kk, k_mod=k_mod, beta=a * kk)
        return groups[g]

    lo16, hi16, same_head = _pair_masks()
    bd = lambda x16: _block_diag_pair(x16, lo16, hi16)
    row_c = lax.broadcasted_iota(jnp.int32, (CHUNK, PAIR), 0)
    lane_c = lax.broadcasted_iota(jnp.int32, (CHUNK, PAIR), 1)
    eye_ls = ((lane_c == row_c) | (lane_c == row_c + HEAD_DIM)).astype(_F32)
    eye = (lax.broadcasted_iota(jnp.int32, (PAIR, PAIR), 0)
           == lax.broadcasted_iota(jnp.int32, (PAIR, PAIR), 1)).astype(_F32)
    zeros16 = jnp.zeros((PAIR, PAIR), jnp.bfloat16)
    zeros_c16 = jnp.zeros((CHUNK, PAIR), jnp.bfloat16)
    m_strict = mask_ref[0, 0]
    m_incl2 = jnp.concatenate([mask_ref[0, 1], mask_ref[0, 1]], axis=1)
    m_neg_strict = mask_ref[0, 2 + len(CHUNK_LEVELS)]
    lv_masks = [mask_ref[0, 2 + lv] for lv in range(len(CHUNK_LEVELS))]
    cum = cum_ref[0]
    lanes = lambda p: slice(p * PAIR, (p + 1) * PAIR)
    pairs = range(N_PAIR)
    results = []

    def program(bb):
        tok = group_prep(bb // group_size)
        rows = slice((bb % group_size) * CHUNK, (bb % group_size + 1) * CHUNK)
        lw_c = tok["lw"][rows]
        g_incl = _dot_split(cum, lw_c)
        yield
        g_excl = g_incl - lw_c
        g_mid = g_incl[CHUNK // 2:CHUNK // 2 + 1, :]
        g_end = jnp.sum(lw_c, axis=0, keepdims=True)
        e_in = jnp.exp(g_incl - g_mid)
        e_out = jnp.exp(g_mid - g_incl)
        e_end = jnp.exp(g_end - g_incl)
        p_end = jnp.exp(g_end)
        p_mid = jnp.exp(g_mid)
        kkd = tok["kk"][rows] * jnp.exp(g_excl - g_mid)
        rd = tok["r"][rows] * e_in
        k_c, b_c = tok["k_mod"][rows], tok["beta"][rows]
        ch = {name: _bf16(val) for name, val in dict(
            kkd=kkd, kkd_n=kkd * (-p_mid), rd=rd, ki=k_c * e_out, bi=b_c * e_out,
            kd=k_c * e_end, bd=b_c * e_end, v=tok["v"][rows]).items()}
        nat = lambda p, name: ch[name][:, lanes(p)]
        rd_abs = rd * p_mid

        gm = [_dot_nt(jnp.concatenate([nat(p, "kkd"), nat(p, "rd")], axis=0),
                      jnp.concatenate([bd(nat(p, "ki")), bd(nat(p, "bi"))], axis=0))
              for p in pairs]
        yield
        g16 = [_bf16(g) for g in gm]
        a_kn = [g[:CHUNK, :PAIR] * m_neg_strict for g in g16]
        a_b = [g[:CHUNK, PAIR:] * m_strict for g in g16]
        r_kb = [g[CHUNK:, :] * m_incl2 for g in g16]
        v_bd = [bd(nat(p, "v")) for p in pairs]
        t_inv = [eye_ls - (a_b[p] * lv_masks[0]).astype(_F32) for p in pairs]
        w0n = [_dot(a_kn[p], v_bd[p]) for p in pairs]
        for lv in range(1, len(CHUNK_LEVELS)):
            t16 = [_bf16(t) for t in t_inv]
            x = [_dot(t16[p], bd(a_b[p] * lv_masks[lv])) for p in pairs]
            yield
            t_new = [_dot(x[p], bd(t16[p])) for p in pairs]
            yield
            t_inv = [t_inv[p] - t_new[p] for p in pairs]
        twk = [_dot(t_inv[p], jnp.concatenate([bd(_bf16(w0n[p])), bd(nat(p, "kkd_n"))], axis=1)) for p in pairs]
        yield
        t16 = [_bf16(t) for t in twk]
        z = [jnp.concatenate([jnp.concatenate([v_bd[p], zeros16], axis=1),
                              jnp.concatenate([bd(t16[p][:, :PAIR]), bd(t16[p][:, PAIR:])], axis=1)], axis=0)
             for p in pairs]
        yr = [_dot(r_kb[p], z[p]) for p in pairs]
        mn = [_dot_tn(jnp.concatenate([nat(p, "kd"), nat(p, "bd")], axis=0),
                      jnp.concatenate([jnp.concatenate([nat(p, "v"), zeros_c16], axis=1), t16[p]], axis=0))
              for p in pairs]
        yield
        st = [state_ref[bb * N_PAIR + p] for p in pairs]
        y_new = [yr[p][:, :PAIR] + _dot(rd_abs[:, lanes(p)] + yr[p][:, PAIR:], st[p]) for p in pairs]
        st_new = [_dot(mn[p][:, PAIR:] * same_head + eye * p_end[:, lanes(p)], st[p])
                  + mn[p][:, :PAIR] * same_head for p in pairs]
        results.append((bb, y_new, st_new))

    def finish():
        for bb, y_new, st_new in results:
            for p in pairs:
                y_ref[0, bb, :, lanes(p)] = y_new[p]
                state_ref[bb * N_PAIR + p] = st_new[p]

    return [program(bb) for bb in range(n_bb)], finish


def _hgrn_programs(q_ref, i_ref, z_ref, lb_ref, mask_ref, sum_ref, o_ref, state_ref, *, layer, n_bb):
    j = pl.program_id(2)

    @pl.when(j == 0)
    def _():
        state_ref[...] = jnp.zeros_like(state_ref)

    logits = lb_ref[0]
    e = jnp.exp(logits - jnp.max(logits, axis=0, keepdims=True))
    lb_w = e / jnp.sum(e, axis=0, keepdims=True)
    csum = lb_w[0:1]
    for l in range(1, layer + 1):
        csum = csum + lb_w[l:l + 1]
    lb = jnp.maximum(csum - lb_w[0:1], 0.0)
    log_lb = jnp.log(lb)
    log_1m_lb = jnp.log1p(-lb)

    lo16, hi16, same_head = _pair_masks()
    bd = lambda x16: _block_diag_pair(x16, lo16, hi16)
    n_lv = len(CHUNK_LEVELS)
    lv_mask = [mask_ref[0, n_lv]] + [mask_ref[0, lv] for lv in range(n_lv)]
    lanes = lambda p: slice(p * PAIR, (p + 1) * PAIR)
    sum_table = sum_ref[0]
    pairs = range(N_PAIR)
    results = []

    def program(bb):
        z = z_ref[bb]
        log_rest = log_1m_lb - _softplus(-z)
        logf = jnp.maximum(log_lb, log_rest) + jnp.log1p(jnp.exp(-jnp.abs(log_lb - log_rest)))
        k_c = (1.0 - lb) * jax.nn.sigmoid(-z)
        q_c = q_ref[bb]
        i_c = i_ref[bb]
        sums = _dot_split(sum_table, logf)
        yield
        g_incl = sums[:CHUNK]
        g_end = jnp.min(g_incl, axis=0, keepdims=True)
        p_end = jnp.exp(g_end)
        qd = q_c * jnp.exp(g_incl)
        kd = k_c * jnp.exp(g_end - g_incl)
        st = [state_ref[bb * N_PAIR + p] for p in pairs]
        inter = [_dot_nt(qd[:, lanes(p)], st[p]) for p in pairs]
        outer = [_dot_tn(i_c[:, lanes(p)], kd[:, lanes(p)]) for p in pairs]
        lv_q, lv_k = [_bf16(q_c)], [_bf16(k_c)]
        for lv in range(n_lv):
            x = jnp.exp(sums[(1 + lv) * CHUNK:(2 + lv) * CHUNK])
            lv_q.append(_bf16(q_c * x))
            lv_k.append(_bf16(k_c * x))
        i16 = _bf16(i_c)
        scores = []
        for p in pairs:
            parts = [_dot_nt(q_l[:, lanes(p)], bd(k_l[:, lanes(p)])) for q_l, k_l in zip(lv_q, lv_k)]
            yield
            sc = parts[0] * lv_mask[0]
            for part, m_l in zip(parts[1:], lv_mask[1:]):
                sc = sc + part * m_l
            scores.append(sc)
        intra = [_dot(scores[p], bd(i16[:, lanes(p)])) for p in pairs]
        st_new = [st[p] * p_end[:, lanes(p)] + outer[p] * same_head for p in pairs]
        yield
        results.append((bb, [intra[p] + inter[p] for p in pairs], st_new))

    def finish():
        for bb, o_new, st_new in results:
            for p in pairs:
                o_ref[0, bb, :, lanes(p)] = o_new[p]
                state_ref[bb * N_PAIR + p] = st_new[p]

    return [program(bb) for bb in range(n_bb)], finish


N_RWKV_IN = 26
N_HGRN_IN = 6


def _scan_kernel(*refs, n_chunk, n_bb, layer):
    rw_in = refs[:N_RWKV_IN]
    hg_in = refs[N_RWKV_IN:N_RWKV_IN + N_HGRN_IN]
    y_ref, bonus_ref, o_ref, rw_state, hg_state = refs[N_RWKV_IN + N_HGRN_IN:]
    rw_programs, rw_finish = _rwkv_programs(*rw_in, y_ref, bonus_ref, rw_state, n_chunk=n_chunk, n_bb=n_bb)
    hg_programs, hg_finish = _hgrn_programs(*hg_in, o_ref, hg_state, layer=layer, n_bb=n_bb)
    programs = [prog for pair in zip(rw_programs, hg_programs) for prog in pair]
    _run_wavefront(programs)
    rw_finish()
    hg_finish()


def _mixer_scan(p_all, params, lb_logits, tables, layer):
    batch, n_tok, _ = p_all.shape
    n_bb = _batch_block(batch)
    n_chunk = n_tok // CHUNK
    n_halo = CHUNK // HALO
    head0 = COL_HEADS // D_RWKV
    lora0 = COL_LORA // PAIR
    chunk_of = _scan_chunk(n_chunk)

    def main(col_of, width=D_RWKV):
        return pl.BlockSpec((n_bb, CHUNK, width), lambda b, d, j: (b, chunk_of(d, j), col_of(d)))

    def prev(col_of, width=D_RWKV):
        return pl.BlockSpec(
            (n_bb, HALO, width),
            lambda b, d, j: (b, jnp.maximum(chunk_of(d, j) * n_halo - 1, 0), col_of(d)))

    def nxt(col_of, width=D_RWKV):
        return pl.BlockSpec(
            (n_bb, HALO, width),
            lambda b, d, j: (b, jnp.minimum((chunk_of(d, j) + 1) * n_halo, n_tok // HALO - 1), col_of(d)))

    col = lambda c: (lambda d: c)
    cols = [col(head0), col(head0 + 1), col(head0 + 2)]
    lora_col = lambda d: lora0 + d
    full = lambda shape: pl.BlockSpec(shape, lambda b, d, j: (0,) * len(shape))
    per_dir = lambda shape: pl.BlockSpec((1,) + shape, lambda b, d, j: (d,) + (0,) * len(shape))
    out_spec = pl.BlockSpec((1, n_bb, CHUNK, D_RWKV), lambda b, d, j: (d, b, chunk_of(d, j), 0))

    rw_masks, hg_masks, cums, hg_sums = tables
    rw_masks = rw_masks.astype(jnp.bfloat16)
    rwkv_specs = (
        [main(c) for c in cols] + [main(lora_col, PAIR)]
        + [prev(c) for c in cols] + [prev(lora_col, PAIR)]
        + [nxt(c) for c in cols] + [nxt(lora_col, PAIR)]
        + [
            full((1, 3, D_RWKV)), full((1, 3, D_RWKV)), full((1, 3, D_RWKV)), per_dir((3, PAIR)),
            per_dir((1, D_RWKV)), per_dir((PAIR, D_RWKV)), per_dir((1, D_RWKV)), per_dir((PAIR, D_RWKV)),
            full((1, D_RWKV)), full((1, D_RWKV)), per_dir((1, D_RWKV)),
            full((D_RWKV, D_RWKV)), per_dir(rw_masks.shape[1:]), per_dir(cums.shape[1:]),
        ])
    hgrn_specs = [
        main(col(head0 + 3)), main(col(head0 + 4)), main(lambda d: head0 + 5 + d),
        per_dir(lb_logits.shape[1:]), per_dir(hg_masks.shape[1:]), per_dir(hg_sums.shape[1:]),
    ]
    assert len(rwkv_specs) == N_RWKV_IN and len(hgrn_specs) == N_HGRN_IN
    out_struct = jax.ShapeDtypeStruct((2, batch, n_tok, D_RWKV), _F32)
    return pl.pallas_call(
        functools.partial(_scan_kernel, n_chunk=n_chunk, n_bb=n_bb, layer=layer),
        grid=(batch // n_bb, 2, n_chunk),
        in_specs=rwkv_specs + hgrn_specs,
        out_specs=[out_spec, out_spec, out_spec],
        out_shape=[out_struct, out_struct, out_struct],
        scratch_shapes=[pltpu.VMEM((n_bb * N_PAIR, PAIR, PAIR), _F32),
                        pltpu.VMEM((n_bb * N_PAIR, PAIR, PAIR), _F32)],
        compiler_params=pltpu.CompilerParams(
            dimension_semantics=("parallel", "arbitrary", "arbitrary"),
            vmem_limit_bytes=VMEM_LIMIT_BYTES),
        name="mixer_scan",
    )(p_all, p_all, p_all, p_all, p_all, p_all, p_all, p_all, p_all, p_all, p_all, p_all,
      params["shift_r"], params["shift_k"], params["shift_v"], params["shift_lora"],
      params["w0"], params["w_up"], params["a0"], params["a_up"],
      params["k_k"], params["k_a"], params["r_k"],
      _head_mean_table(), rw_masks, cums,
      p_all, p_all, p_all, lb_logits, hg_masks, hg_sums)


def _readout_kernel(*refs, alpha, off, n_rb):
    (*h_refs, gate_ref, pool_ref, yf_ref, yb_ref, bf_ref, bb_ref, of_ref, ob_ref,
     gt_ref, pm_ref, pw_ref, ps_ref, gng_ref, gnb_ref, ng_ref, hm_ref,
     wout_ref, lng_ref, lnb_ref, out_ref) = refs
    group = lax.broadcasted_iota(jnp.int32, (TOK_BLOCK, D_POOL), 1) // POOL_GROUP

    def program(rb):
        pv = pool_ref[rb]
        pools = [_dot(pm_ref[0, g], pv) for g in range(len(POOL_WINDOWS))]
        y = yf_ref[0, rb] + yb_ref[0, rb]
        y_mean = _dot(y, hm_ref[...])
        o = of_ref[0, rb] + ob_ref[0, rb]
        o_ms = _dot(o * o, hm_ref[...])
        yield
        pooled = jnp.where(group == 0, pools[0], 0.0)
        for g in range(1, len(POOL_WINDOWS)):
            pooled = pooled + jnp.where(group == g, pools[g], 0.0)
        mixed = _dot(pooled, pw_ref[...])
        yc = y - y_mean
        var = _dot(yc * yc, hm_ref[...])
        yield
        rw = yc * lax.rsqrt(var + GN_EPS) * gng_ref[...] + gnb_ref[...] + bf_ref[0, rb] + bb_ref[0, rb]
        ho = o * lax.rsqrt(o_ms + RMS_EPS) * ng_ref[...]
        gate = gate_ref[rb]
        mix = jnp.concatenate([mixed * ps_ref[...], rw, ho], axis=1) * (gate * jax.nn.sigmoid(gate))
        proj = jnp.dot(mix.astype(jnp.bfloat16), wout_ref[...], preferred_element_type=_F32)
        yield
        h = _stream_block(h_refs, off, rb)
        out_ref[rb] = _layer_norm(alpha * h + gt_ref[rb] * proj) * lng_ref[...] + lnb_ref[...]

    _run_wavefront([program(rb) for rb in range(n_rb)])


def _readout(stream, p_all, y, bonus, o, gate_mod, params, alpha, skip_ctx):
    batch, n_tok, _ = p_all.shape
    n_blk = n_tok // TOK_BLOCK
    n_rb = math.gcd(batch, READOUT_ROWS)
    off = 1 if skip_ctx else 0
    tok = lambda width, col: pl.BlockSpec((n_rb, TOK_BLOCK, width), lambda b, j: (b, j + off, col))
    dirs = lambda d: pl.BlockSpec((1, n_rb, TOK_BLOCK, D_RWKV), lambda b, j: (d, b, j + off, 0))
    full = lambda shape: pl.BlockSpec(shape, lambda b, j: (0,) * len(shape))
    n_win = len(POOL_WINDOWS)
    gate_mod = jnp.concatenate([gate_mod[:batch]] + [gate_mod[batch:]] * n_rb, axis=0)
    return pl.pallas_call(
        functools.partial(_readout_kernel, alpha=alpha, off=off, n_rb=n_rb),
        grid=(batch // n_rb, n_blk - off),
        in_specs=_stream_specs(stream, off, n_rb) + [
            tok(D_MODEL, COL_GATE // D_MODEL), tok(D_POOL, COL_POOL // D_POOL),
            dirs(0), dirs(1), dirs(0), dirs(1), dirs(0), dirs(1),
            pl.BlockSpec((n_rb, 1, D_MODEL), lambda b, j: (jnp.where(j + off == 0, batch // n_rb, b), 0, 0)),
            pl.BlockSpec((1, n_win, TOK_BLOCK, TOK_BLOCK),
                         lambda b, j: (jnp.where(j + off == 0, 0, 1), 0, 0, 0)),
            full((D_POOL, D_POOL)), full((1, D_POOL)),
            full((1, D_RWKV)), full((1, D_RWKV)), full((1, D_HGRN)), full((D_RWKV, D_RWKV)),
            full((D_MODEL, D_MODEL)), full((1, D_MODEL)), full((1, D_MODEL)),
        ],
        out_specs=pl.BlockSpec((n_rb, TOK_BLOCK, D_MODEL), lambda b, j: (b, j, 0)),
        out_shape=jax.ShapeDtypeStruct((batch, n_tok - off * TOK_BLOCK, D_MODEL), _F32),
        compiler_params=pltpu.CompilerParams(
            dimension_semantics=("parallel", "parallel"), vmem_limit_bytes=VMEM_LIMIT_BYTES),
        name="readout",
    )(*stream, p_all, p_all, y, y, bonus, bonus, o, o, gate_mod,
      jnp.asarray(_pool_tables()), params["pool_w"], params["pool_scale"],
      params["gn_g"], params["gn_b"], params["norm_g"], _head_mean_table(),
      params["w_out"], params["ln_g"], params["ln_b"])


def _block_diag(blocks):
    n, a, b = blocks.shape
    out = jnp.zeros((n * a, n * b), blocks.dtype)
    for g in range(n):
        out = out.at[g * a:(g + 1) * a, g * b:(g + 1) * b].set(blocks[g])
    return out


def _permute_columns(w):
    perm = _column_perm()
    cuts = [0] + [i for i in range(1, len(perm)) if perm[i] != perm[i - 1] + 1] + [len(perm)]
    return jnp.concatenate([w[:, int(perm[a]):int(perm[b - 1]) + 1] for a, b in zip(cuts[:-1], cuts[1:])], axis=1)


def _layer_params(l, w_in, rwkv_shift, pool_w, pool_scale, rwkv_w0, rwkv_w_up, rwkv_a0, rwkv_a_up,
                  rwkv_k_k, rwkv_k_a, rwkv_r_k, rwkv_gn_g, rwkv_gn_b, hgrn_norm_g, w_out, ln_g, ln_b):
    sh = rwkv_shift[l]
    lora_sh = jnp.stack([
        jnp.concatenate([sh[:, 1152 + d * LORA:1152 + (d + 1) * LORA],
                         sh[:, 1280 + d * LORA:1280 + (d + 1) * LORA]], axis=1) for d in range(2)])
    pad = jnp.zeros((2, LORA, D_RWKV), _F32)
    return {
        "w_in": _permute_columns(w_in[l]).astype(jnp.bfloat16),
        "shift_r": sh[None, :, 0:384], "shift_k": sh[None, :, 384:768], "shift_v": sh[None, :, 768:1152],
        "shift_lora": lora_sh,
        "w0": rwkv_w0[l][:, None, :], "a0": rwkv_a0[l][:, None, :],
        "w_up": jnp.concatenate([rwkv_w_up[l], pad], axis=1),
        "a_up": jnp.concatenate([pad, rwkv_a_up[l]], axis=1),
        "k_k": rwkv_k_k[l][None], "k_a": rwkv_k_a[l][None],
        "r_k": rwkv_r_k[l].reshape(2, 1, D_RWKV),
        "pool_w": _block_diag(pool_w[l]), "pool_scale": pool_scale[l][None],
        "gn_g": rwkv_gn_g[l][None], "gn_b": rwkv_gn_b[l][None], "norm_g": hgrn_norm_g[l][None],
        "w_out": w_out[l].astype(jnp.bfloat16), "ln_g": ln_g[l][None], "ln_b": ln_b[l][None],
    }


def kernel(x, c, ctx, c_ctx, mod_w, mod_b, w_in, rwkv_shift, pool_w, pool_scale, rwkv_w0, rwkv_w_up,
           rwkv_a0, rwkv_a_up, rwkv_k_k, rwkv_k_a, rwkv_r_k, rwkv_gn_g, rwkv_gn_b, hgrn_lb_logits,
           hgrn_norm_g, w_out, ln_g, ln_b):
    depth = mod_w.shape[0]
    batch = x.shape[0]
    assert ctx.shape[1] == CTX_LEN == TOK_BLOCK and x.shape[1] % TOK_BLOCK == 0
    alpha = float((2 * depth) ** 0.25)
    tables = tuple(jnp.asarray(t) for t in _scan_tables())

    c_all = jnp.concatenate([c.astype(_F32), c_ctx.astype(_F32)[None]], axis=0)
    mod = _modulation(c_all, mod_w, mod_b)
    stream = (ctx.astype(_F32), x.astype(_F32))

    for l in range(depth):
        prm = _layer_params(l, w_in, rwkv_shift, pool_w, pool_scale, rwkv_w0, rwkv_w_up, rwkv_a0,
                            rwkv_a_up, rwkv_k_k, rwkv_k_a, rwkv_r_k, rwkv_gn_g, rwkv_gn_b,
                            hgrn_norm_g, w_out, ln_g, ln_b)
        shift = mod[l, :, None, 0:D_MODEL]
        scale = mod[l, :, None, D_MODEL:2 * D_MODEL]
        gate_mod = mod[l, :, None, 2 * D_MODEL:]
        p_all = _input_projection(stream, shift, scale, prm["w_in"])
        y, bonus, o = _mixer_scan(p_all, prm, hgrn_lb_logits, tables, l)
        stream = (_readout(stream, p_all, y, bonus, o, gate_mod, prm, alpha, skip_ctx=(l == depth - 1)),)
    return stream[0].astype(x.dtype)
```
